```python
import math
import jax, jax.numpy as jnp
from jax import lax
import numpy as np

D_MODEL = 1024
BATCH = 8
SEQ = 4096
DEPTH = 1
DEC_BATCH = 32
DEC_SEQ = 64
PAST_LEN = 4096

CHUNK = 64
SSM_EXPAND = 2
D_INNER = SSM_EXPAND * D_MODEL
SSM_HEAD_DIM = 64
SSM_HEADS = D_INNER // SSM_HEAD_DIM
N_GROUPS = 4
HEADS_PER_GROUP = SSM_HEADS // N_GROUPS
D_STATE = 128
CONV_W = 4
CONV_DIM = D_INNER + 2 * N_GROUPS * D_STATE
SSD_CHUNK = 64
ATT_HEADS = 16
ATT_HEAD_DIM = 64
D_ATT = ATT_HEADS * ATT_HEAD_DIM
LEFT_CHUNKS = 8
ATT_WINDOW = LEFT_CHUNKS * CHUNK
BAND = ATT_WINDOW + CHUNK
REL_CLIP = 256
SPLIT_SIZES = (D_INNER, CONV_DIM, SSM_HEADS, D_ATT, D_ATT, D_ATT, D_ATT, D_MODEL, D_MODEL)
D_IN_PROJ = sum(SPLIT_SIZES)
NORM_EPS = 1e-5
NEG_INF = -1e30

kernel_name = 'hybrid_ssd_chunkband_attn_stream_step'


def rmsnorm(x, w):
    xf = x.astype(jnp.float32)
    xf = xf * lax.rsqrt(jnp.mean(xf * xf, axis=-1, keepdims=True) + NORM_EPS)
    return xf.astype(x.dtype) * w


def split_projection(x, norm_w, w_in):
    xn = rmsnorm(x, norm_w)
    proj = jnp.einsum('bld,de->ble', xn, w_in)
    points, acc = [], 0
    for s in SPLIT_SIZES[:-1]:
        acc += s
        points.append(acc)
    return jnp.split(proj, points, axis=-1)


def causal_conv(u, buf, conv_w, conv_b):
    L = u.shape[1]
    up = jnp.concatenate([buf.astype(u.dtype), u], axis=1)
    y = conv_b
    for j in range(CONV_W):
        y = y + conv_w[j] * up[:, j:j + L]
    return jax.nn.silu(y), up[:, -(CONV_W - 1):]


def ssd_scan(xdt, adt, bm, cm, h0):
    b, L = xdt.shape[:2]
    n_chunks = -(-L // SSD_CHUNK)
    pad = n_chunks * SSD_CHUNK - L

    def blocks(t):
        t = jnp.pad(t, [(0, 0), (0, pad)] + [(0, 0)] * (t.ndim - 2))
        t = t.reshape((b, n_chunks, SSD_CHUNK) + t.shape[2:])
        return jnp.moveaxis(t, 1, 0)

    tri = jnp.tril(jnp.ones((SSD_CHUNK, SSD_CHUNK), bool))[None, :, :, None, None]

    def step(h, inp):
        xc, ac, bc, cc = inp
        acs = jnp.cumsum(ac, axis=1)
        seg = acs[:, :, None] - acs[:, None, :]
        decay = jnp.exp(jnp.where(tri, seg, -jnp.inf))
        cb = jnp.einsum('blgn,bsgn->blsg', cc, bc)
        y_diag = jnp.einsum('blsgr,bsgrp->blgrp', cb[..., None] * decay, xc)
        y_off = jnp.einsum('blgn,bgrpn->blgrp', cc, h) * jnp.exp(acs)[..., None]
        to_end = jnp.exp(acs[:, -1:] - acs)
        h_new = h * jnp.exp(acs[:, -1])[..., None, None] + jnp.einsum(
            'blgn,blgrp->bgrpn', bc, xc * to_end[..., None])
        return h_new, y_diag + y_off

    h_final, ys = lax.scan(step, h0, (blocks(xdt), blocks(adt), blocks(bm), blocks(cm)))
    ys = jnp.moveaxis(ys, 0, 1).reshape((b, n_chunks * SSD_CHUNK) + ys.shape[3:])[:, :L]
    return ys, h_final


def ssm_branch(z, xbc, dt_raw, conv_buf, h0, conv_w, conv_b, dt_bias, a_log, d_skip,
               ssm_norm_w, w_out_ssm):
    b, L, _ = z.shape
    f32 = jnp.float32
    xbc_c, new_buf = causal_conv(xbc, conv_buf, conv_w, conv_b)
    xs, bm, cm = jnp.split(xbc_c, [D_INNER, D_INNER + N_GROUPS * D_STATE], axis=-1)
    xs = xs.astype(f32).reshape(b, L, N_GROUPS, HEADS_PER_GROUP, SSM_HEAD_DIM)
    bm = bm.astype(f32).reshape(b, L, N_GROUPS, D_STATE)
    cm = cm.astype(f32).reshape(b, L, N_GROUPS, D_STATE)
    dt = jax.nn.softplus((dt_raw + dt_bias).astype(f32)).reshape(b, L, N_GROUPS, HEADS_PER_GROUP)
    a = -jnp.exp(a_log.astype(f32)).reshape(N_GROUPS, HEADS_PER_GROUP)
    h0 = h0.astype(f32).reshape(b, N_GROUPS, HEADS_PER_GROUP, SSM_HEAD_DIM, D_STATE)
    y, h_final = ssd_scan(xs * dt[..., None], dt * a, bm, cm, h0)
    y = y + d_skip.astype(f32).reshape(N_GROUPS, HEADS_PER_GROUP)[:, :, None] * xs
    yz = y.reshape(b, L, D_INNER) * jax.nn.silu(z.astype(f32))
    yz = yz.reshape(b, L, N_GROUPS, D_INNER // N_GROUPS)
    yz = yz * lax.rsqrt(jnp.mean(yz * yz, axis=-1, keepdims=True) + NORM_EPS)
    yz = yz.reshape(b, L, D_INNER).astype(z.dtype) * ssm_norm_w
    y_out = jnp.einsum('bld,de->ble', yz, w_out_ssm)
    h_final = h_final.reshape(b, SSM_HEADS, SSM_HEAD_DIM, D_STATE).astype(z.dtype)
    return y_out, new_buf, h_final


def rel_bias_matrix(rel_bias, n_q, n_k, offset):
    d = offset + jnp.arange(n_q)[:, None] - jnp.arange(n_k)[None, :]
    idx = jnp.clip(d, -REL_CLIP, REL_CLIP) + REL_CLIP
    return rel_bias[:, idx].astype(jnp.float32)


def band_attention(q, k, v, bias, valid):
    s = jnp.einsum('bthd,bshd->bhts', q, k, preferred_element_type=jnp.float32)
    s = s * (ATT_HEAD_DIM ** -0.5) + bias
    if valid is not None:
        s = jnp.where(valid, s, NEG_INF)
    p = jax.nn.softmax(s, axis=-1)
    return jnp.einsum('bhts,bshd->bthd', p.astype(v.dtype), v)


def chunk_band_attention_prompt(q, k, v, rel_bias):
    b, L = q.shape[:2]
    n_chunks = L // CHUNK
    kp = jnp.pad(k, ((0, 0), (ATT_WINDOW, 0), (0, 0), (0, 0)))
    vp = jnp.pad(v, ((0, 0), (ATT_WINDOW, 0), (0, 0), (0, 0)))
    bias = rel_bias_matrix(rel_bias, CHUNK, BAND, ATT_WINDOW)
    j_idx = jnp.arange(BAND)

    def one_chunk(c):
        start = c * CHUNK
        qc = lax.dynamic_slice_in_dim(q, start, CHUNK, axis=1)
        kc = lax.dynamic_slice_in_dim(kp, start, BAND, axis=1)
        vc = lax.dynamic_slice_in_dim(vp, start, BAND, axis=1)
        valid = (start - ATT_WINDOW + j_idx >= 0)[None, :]
        return band_attention(qc, kc, vc, bias, valid)

    o = lax.map(one_chunk, jnp.arange(n_chunks))
    o = jnp.moveaxis(o, 0, 1).reshape(b, L, D_ATT)
    keep = min(ATT_WINDOW, L)
    return o, k[:, -keep:], v[:, -keep:]


def chunk_band_attention_step(q, k, v, k_past, v_past, rel_bias):
    b, T = q.shape[:2]
    w = k_past.shape[1]
    kk = jnp.concatenate([k_past.astype(k.dtype), k], axis=1)
    vv = jnp.concatenate([v_past.astype(v.dtype), v], axis=1)
    bias = rel_bias_matrix(rel_bias, T, w + T, w)
    o = band_attention(q, kk, vv, bias, None).reshape(b, T, D_ATT)
    return o, kk[:, -w:], vv[:, -w:]


def hybrid_layer(x, h0, conv_buf, k_past, v_past, norm_w, w_in, conv_w, conv_b, dt_bias,
                 a_log, d_skip, ssm_norm_w, w_out_ssm, rel_bias, w_out_att, w_o):
    b, L, _ = x.shape
    z, xbc, dt_raw, q, k, v, g_att, gate_ssm, gate_att = split_projection(x, norm_w, w_in)
    y_ssm, new_conv, new_h = ssm_branch(z, xbc, dt_raw, conv_buf, h0, conv_w, conv_b, dt_bias,
                                        a_log, d_skip, ssm_norm_w, w_out_ssm)
    q = q.reshape(b, L, ATT_HEADS, ATT_HEAD_DIM)
    k = k.reshape(b, L, ATT_HEADS, ATT_HEAD_DIM)
    v = v.reshape(b, L, ATT_HEADS, ATT_HEAD_DIM)
    if k_past is None:
        o, new_k, new_v = chunk_band_attention_prompt(q, k, v, rel_bias)
    else:
        o, new_k, new_v = chunk_band_attention_step(q, k, v, k_past, v_past, rel_bias)
    y_att = jnp.einsum('bld,de->ble', o * jax.nn.silu(g_att), w_out_att)
    merged = jax.nn.sigmoid(gate_ssm) * y_ssm + jax.nn.sigmoid(gate_att) * y_att
    x = x + jnp.einsum('bld,de->ble', merged, w_o)
    return x, new_h, new_conv, new_k, new_v


def setup_inputs(seed: int = 0) -> dict:
    key = jax.random.key(seed)
    ks = jax.random.split(key, 20)
    f32 = jnp.float32
    att_cache = min(ATT_WINDOW, PAST_LEN)

    def nrm(k, shape, scale):
        return scale * jax.random.normal(k, shape, f32)

    dt = jnp.exp(jax.random.uniform(ks[10], (DEPTH, SSM_HEADS), f32,
                                    math.log(1e-3), math.log(1e-1)))
    return {
        'x_prompt': nrm(ks[0], (BATCH, SEQ, D_MODEL), 1.0),
        'x_sample': nrm(ks[1], (DEC_BATCH, DEC_SEQ, D_MODEL), 1.0),
        'state_ssm': nrm(ks[2], (DEPTH, DEC_BATCH, SSM_HEADS, SSM_HEAD_DIM, D_STATE), 0.1),
        'state_conv': nrm(ks[3], (DEPTH, DEC_BATCH, CONV_W - 1, CONV_DIM), 1.0),
        'cache_k': nrm(ks[4], (DEPTH, DEC_BATCH, att_cache, ATT_HEADS, ATT_HEAD_DIM), 1.0),
        'cache_v': nrm(ks[5], (DEPTH, DEC_BATCH, att_cache, ATT_HEADS, ATT_HEAD_DIM), 1.0),
        'norm_w': 1.0 + nrm(ks[6], (DEPTH, D_MODEL), 0.01),
        'w_in': nrm(ks[7], (DEPTH, D_MODEL, D_IN_PROJ), D_MODEL ** -0.5),
        'conv_w': nrm(ks[8], (DEPTH, CONV_W, CONV_DIM), CONV_W ** -0.5),
        'conv_b': nrm(ks[9], (DEPTH, CONV_DIM), 0.01),
        'dt_bias': dt + jnp.log(-jnp.expm1(-dt)),
        'a_log': jnp.log(jax.random.uniform(ks[11], (DEPTH, SSM_HEADS), f32, 1.0, 16.0)),
        'd_skip': 1.0 + nrm(ks[12], (DEPTH, SSM_HEADS), 0.1),
        'ssm_norm_w': 1.0 + nrm(ks[13], (DEPTH, D_INNER), 0.01),
        'w_out_ssm': nrm(ks[14], (DEPTH, D_INNER, D_MODEL), D_INNER ** -0.5),
        'rel_bias': nrm(ks[15], (DEPTH, ATT_HEADS, 2 * REL_CLIP + 1), 0.1),
        'w_out_att': nrm(ks[16], (DEPTH, D_ATT, D_MODEL), D_ATT ** -0.5),
        'w_o': nrm(ks[17], (DEPTH, D_MODEL, D_MODEL), D_MODEL ** -0.5),
        'final_norm_w': 1.0 + nrm(ks[18], (D_MODEL,), 0.01),
    }


def reference(x_prompt, x_sample, state_ssm, state_conv, cache_k, cache_v,
              norm_w, w_in, conv_w, conv_b, dt_bias, a_log, d_skip, ssm_norm_w,
              w_out_ssm, rel_bias, w_out_att, w_o, final_norm_w):
    h_p, h_s = x_prompt, x_sample
    bp = x_prompt.shape[0]
    h0_p = jnp.zeros((bp, SSM_HEADS, SSM_HEAD_DIM, D_STATE), x_prompt.dtype)
    conv0_p = jnp.zeros((bp, CONV_W - 1, CONV_DIM), x_prompt.dtype)
    ssm_p, conv_p, k_p, v_p = [], [], [], []
    ssm_s, conv_s, k_s, v_s = [], [], [], []
    for l in range(DEPTH):
        lw = (norm_w[l], w_in[l], conv_w[l], conv_b[l], dt_bias[l], a_log[l], d_skip[l],
              ssm_norm_w[l], w_out_ssm[l], rel_bias[l], w_out_att[l], w_o[l])
        h_p, a1, a2, a3, a4 = hybrid_layer(h_p, h0_p, conv0_p, None, None, *lw)
        ssm_p.append(a1); conv_p.append(a2); k_p.append(a3); v_p.append(a4)
        h_s, b1, b2, b3, b4 = hybrid_layer(h_s, state_ssm[l], state_conv[l],
                                           cache_k[l], cache_v[l], *lw)
        ssm_s.append(b1); conv_s.append(b2); k_s.append(b3); v_s.append(b4)
    y_prompt = rmsnorm(h_p, final_norm_w)
    y_sample = rmsnorm(h_s, final_norm_w)
    new_ssm_p = jnp.stack(ssm_p)
    new_conv_p = jnp.stack(conv_p)
    new_k_p = jnp.stack(k_p)
    new_v_p = jnp.stack(v_p)
    new_ssm_s = jnp.stack(ssm_s)
    new_conv_s = jnp.stack(conv_s)
    new_k_s = jnp.stack(k_s)
    new_v_s = jnp.stack(v_s)
    return (y_prompt, y_sample, new_ssm_p, new_conv_p, new_k_p, new_v_p,
            new_ssm_s, new_conv_s, new_k_s, new_v_s)
```

```python
import functools

import jax
import jax.numpy as jnp
from jax import lax
from jax.experimental import pallas as pl
from jax.experimental.pallas import tpu as pltpu

D_MODEL = 1024
CHUNK = 64
D_INNER = 2048
SSM_HEAD_DIM = 64
SSM_HEADS = 32
N_GROUPS = 4
D_STATE = 128
GROUP_W = D_INNER // N_GROUPS
BC_W = N_GROUPS * D_STATE
CONV_W = 4
CONV_DIM = D_INNER + 2 * BC_W
ATT_HEADS = 16
ATT_HEAD_DIM = 64
D_ATT = 1024
LEFT_CHUNKS = 8
BAND_CHUNKS = LEFT_CHUNKS + 1
BAND = BAND_CHUNKS * CHUNK
REL_CLIP = 256
NORM_EPS = 1e-5
NEG_INF = -1e30

COL_Z = 0
COL_X = 2048
COL_B = 4096
COL_C = 4608
COL_Q = 5120
COL_K = 6144
COL_V = 7168
COL_G = 8192
COL_GS = 9216
COL_GA = 10240
P_COLS = 11264
DT_PAD = 128

VMEM_LIMIT = 56 * 1024 * 1024

F32 = jnp.float32
BF16 = jnp.bfloat16


def _sigmoid(x):
    return 1.0 / (1.0 + jnp.exp(-x))


def _softplus(x):
    return jnp.maximum(x, 0.0) + jnp.log(1.0 + jnp.exp(-jnp.abs(x)))


PROJ_TM = 1024
PROJ_TN = 1024


def _proj_kernel(x_ref, nw_ref, w_ref, wdt_ref, p_ref, dt_ref, xn_ref):
    @pl.when(pl.program_id(1) == 0)
    def _():
        x = x_ref[...]
        ms = jnp.mean(x * x, axis=-1, keepdims=True)
        xn = (x * lax.rsqrt(ms + NORM_EPS)) * nw_ref[...]
        xn_ref[...] = xn.astype(BF16)
        dt_ref[...] = jnp.dot(xn_ref[...], wdt_ref[...], preferred_element_type=F32)

    p_ref[...] = jnp.dot(xn_ref[...], w_ref[...], preferred_element_type=F32).astype(BF16)


def _proj(x2d, norm_w, w_main, w_dt):
    n = x2d.shape[0]
    tm = min(PROJ_TM, n)
    return pl.pallas_call(
        _proj_kernel,
        grid=(n // tm, P_COLS // PROJ_TN),
        in_specs=[
            pl.BlockSpec((tm, D_MODEL), lambda i, j: (i, 0)),
            pl.BlockSpec((1, D_MODEL), lambda i, j: (0, 0)),
            pl.BlockSpec((D_MODEL, PROJ_TN), lambda i, j: (0, j)),
            pl.BlockSpec((D_MODEL, DT_PAD), lambda i, j: (0, 0)),
        ],
        out_specs=[
            pl.BlockSpec((tm, PROJ_TN), lambda i, j: (i, j)),
            pl.BlockSpec((tm, DT_PAD), lambda i, j: (i, 0)),
        ],
        out_shape=[
            jax.ShapeDtypeStruct((n, P_COLS), BF16),
            jax.ShapeDtypeStruct((n, DT_PAD), F32),
        ],
        scratch_shapes=[pltpu.VMEM((tm, D_MODEL), BF16)],
        compiler_params=pltpu.CompilerParams(
            dimension_semantics=("arbitrary", "arbitrary"),
            vmem_limit_bytes=VMEM_LIMIT),
        name="proj",
    )(x2d, norm_w, w_main, w_dt)


def _conv_silu(u_bf16, tail, w, b):
    u = u_bf16.astype(F32)
    row8 = lax.broadcasted_iota(jnp.int32, (8, 1), 0)
    acc = b + w[CONV_W - 1:CONV_W, :] * u
    for s in range(1, CONV_W):
        us = pltpu.roll(u, s, axis=0)
        ts = pltpu.roll(tail, s, axis=0)
        head = jnp.where(row8 < s, ts, us[0:8, :])
        us = jnp.concatenate([head, us[8:, :]], axis=0)
        acc = acc + w[CONV_W - 1 - s:CONV_W - s, :] * us
    return acc * _sigmoid(acc), u[CHUNK - 8:, :]


def _ssd_kernel(z_ref, x_ref, b_ref, c_ref, dtr_ref, h0_ref, conv0_ref,
                cw_ref, cb_ref, dtb_ref, alog_ref, dskip_ref, nw_ref,
                yz_ref, hout_ref, tailout_ref,
                ht_ref, tail_ref, xw_ref, y_ref, ea_ref):
    c = pl.program_id(1)
    n_chunks = pl.num_programs(1)

    @pl.when(c == 0)
    def _():
        for g in range(N_GROUPS):
            sl = slice(g * GROUP_W, (g + 1) * GROUP_W)
            ht_ref[:, sl] = h0_ref[0, sl, :].T
        tail_ref[...] = conv0_ref[0]

    xs, tail_x = _conv_silu(x_ref[...], tail_ref[:, 0:D_INNER],
                            cw_ref[:, 0:D_INNER], cb_ref[:, 0:D_INNER])
    bm, tail_b = _conv_silu(b_ref[...], tail_ref[:, D_INNER:D_INNER + BC_W],
                            cw_ref[:, D_INNER:D_INNER + BC_W], cb_ref[:, D_INNER:D_INNER + BC_W])
    cm, tail_c = _conv_silu(c_ref[...], tail_ref[:, D_INNER + BC_W:],
                            cw_ref[:, D_INNER + BC_W:], cb_ref[:, D_INNER + BC_W:])
    tail_ref[:, 0:D_INNER] = tail_x
    tail_ref[:, D_INNER:D_INNER + BC_W] = tail_b
    tail_ref[:, D_INNER + BC_W:] = tail_c
    bm_b = bm.astype(BF16)
    cm_b = cm.astype(BF16)

    dt = _softplus(dtr_ref[...] + dtb_ref[...])
    adt = dt * (-jnp.exp(alog_ref[...]))
    li = lax.broadcasted_iota(jnp.int32, (CHUNK, CHUNK), 0)
    si = lax.broadcasted_iota(jnp.int32, (CHUNK, CHUNK), 1)
    tri = (li >= si).astype(F32)
    acs = jnp.dot(tri, adt, precision=lax.Precision.HIGHEST,
                  preferred_element_type=F32)
    s2 = lax.broadcasted_iota(jnp.int32, (CHUNK, 2 * CHUNK), 0)
    l2 = lax.broadcasted_iota(jnp.int32, (CHUNK, 2 * CHUNK), 1) % CHUNK
    tri_t2 = (l2 >= s2).astype(F32)
    acs_t2 = lax.dot_general(adt, tri_t2, (((0,), (0,)), ((), ())),
                             precision=lax.Precision.HIGHEST,
                             preferred_element_type=F32)

    lane = lax.broadcasted_iota(jnp.int32, (CHUNK, 2 * CHUNK), 1)
    lo = lane < CHUNK
    lo_row = lo[0:1, :]
    tri2 = (lax.broadcasted_iota(jnp.int32, (CHUNK, 2 * CHUNK), 0) >= (lane % CHUNK))

    yoff = []
    cb2 = []
    for g in range(N_GROUPS):
        sl = slice(g * GROUP_W, (g + 1) * GROUP_W)
        ns = slice(g * D_STATE, (g + 1) * D_STATE)
        yoff.append(jnp.dot(cm_b[:, ns], ht_ref[:, sl].astype(BF16),
                            preferred_element_type=F32))
        b2 = jnp.concatenate([bm_b[:, ns], bm_b[:, ns]], axis=0)
        cb2.append(lax.dot_general(cm_b[:, ns], b2, (((1,), (1,)), ((), ())),
                                   preferred_element_type=F32))

    for j in range(SSM_HEADS // 2):
        g = j // (SSM_HEADS // 2 // N_GROUPS)
        r0, r1 = 2 * j, 2 * j + 1
        sl = slice(j * 128, (j + 1) * 128)
        gsl = slice((j % 4) * 128, (j % 4 + 1) * 128)
        u_col = jnp.where(lo, acs[:, r0:r0 + 1], acs[:, r1:r1 + 1])
        v_row = jnp.where(lo_row, acs_t2[r0:r0 + 1, :], acs_t2[r1:r1 + 1, :])
        dtp = jnp.where(lo, dt[:, r0:r0 + 1], dt[:, r1:r1 + 1])
        decay = jnp.exp(jnp.where(tri2, u_col - v_row, -jnp.inf))
        m = (cb2[g] * decay).astype(BF16)
        xs_p = xs[:, sl]
        xdt = xs_p * dtp
        xdt_b = xdt.astype(BF16)
        zero = jnp.zeros_like(xdt_b)
        rhs = jnp.concatenate([jnp.where(lo, xdt_b, zero), jnp.where(lo, zero, xdt_b)], axis=0)
        yd = jnp.dot(m, rhs, preferred_element_type=F32)
        e_u = jnp.exp(u_col)
        y = yd + yoff[g][:, gsl] * e_u + dskip_ref[:, sl] * xs_p
        to_end = jnp.exp(u_col[CHUNK - 1:CHUNK, :] - u_col)
        xw_ref[:, sl] = (xdt * to_end).astype(BF16)
        ea_ref[:, sl] = e_u[CHUNK - 8:CHUNK, :]
        zf = z_ref[:, sl].astype(F32)
        y_ref[:, sl] = y * (zf * _sigmoid(zf))

    for g in range(N_GROUPS):
        sl = slice(g * GROUP_W, (g + 1) * GROUP_W)
        ns = slice(g * D_STATE, (g + 1) * D_STATE)
        yg = y_ref[:, sl]
        ms = jnp.mean(yg * yg, axis=-1, keepdims=True)
        yz_ref[:, sl] = ((yg * lax.rsqrt(ms + NORM_EPS)) * nw_ref[:, sl]).astype(BF16)
        upd = lax.dot_general(bm_b[:, ns], xw_ref[:, sl], (((0,), (0,)), ((), ())),
                              preferred_element_type=F32)
        ht_ref[:, sl] = ht_ref[:, sl] * ea_ref[7:8, sl] + upd

    @pl.when(c == n_chunks - 1)
    def _():
        for g in range(N_GROUPS):
            sl = slice(g * GROUP_W, (g + 1) * GROUP_W)
            hout_ref[0, sl, :] = ht_ref[:, sl].T
        tailout_ref[0] = tail_ref[...]


def _ssd(p, dtraw, h0, conv0, conv_w, conv_b, dt_bias, a_log, d_skip_x, ssm_norm_w, nb, nc):
    n = nb * nc * CHUNK
    row = lambda b, c: b * nc + c
    const = lambda b, c: (0, 0)
    return pl.pallas_call(
        _ssd_kernel,
        grid=(nb, nc),
        in_specs=[
            pl.BlockSpec((CHUNK, D_INNER), lambda b, c: (row(b, c), COL_Z // D_INNER)),
            pl.BlockSpec((CHUNK, D_INNER), lambda b, c: (row(b, c), COL_X // D_INNER)),
            pl.BlockSpec((CHUNK, BC_W), lambda b, c: (row(b, c), COL_B // BC_W)),
            pl.BlockSpec((CHUNK, BC_W), lambda b, c: (row(b, c), COL_C // BC_W)),
            pl.BlockSpec((CHUNK, DT_PAD), lambda b, c: (row(b, c), 0)),
            pl.BlockSpec((1, D_INNER, D_STATE), lambda b, c: (b, 0, 0)),
            pl.BlockSpec((1, 8, CONV_DIM), lambda b, c: (b, 0, 0)),
            pl.BlockSpec((CONV_W, CONV_DIM), const),
            pl.BlockSpec((1, CONV_DIM), const),
            pl.BlockSpec((1, DT_PAD), const),
            pl.BlockSpec((1, DT_PAD), const),
            pl.BlockSpec((1, D_INNER), const),
            pl.BlockSpec((1, D_INNER), const),
        ],
        out_specs=[
            pl.BlockSpec((CHUNK, D_INNER), lambda b, c: (row(b, c), 0)),
            pl.BlockSpec((1, D_INNER, D_STATE), lambda b, c: (b, 0, 0)),
            pl.BlockSpec((1, 8, CONV_DIM), lambda b, c: (b, 0, 0)),
        ],
        out_shape=[
            jax.ShapeDtypeStruct((n, D_INNER), BF16),
            jax.ShapeDtypeStruct((nb, D_INNER, D_STATE), F32),
            jax.ShapeDtypeStruct((nb, 8, CONV_DIM), F32),
        ],
        scratch_shapes=[
            pltpu.VMEM((D_STATE, D_INNER), F32),
            pltpu.VMEM((8, CONV_DIM), F32),
            pltpu.VMEM((CHUNK, D_INNER), BF16),
            pltpu.VMEM((CHUNK, D_INNER), F32),
            pltpu.VMEM((8, D_INNER), F32),
        ],
        compiler_params=pltpu.CompilerParams(
            dimension_semantics=("arbitrary", "arbitrary"),
            vmem_limit_bytes=VMEM_LIMIT),
        name="ssd",
    )(p, p, p, p, dtraw, h0, conv0, conv_w, conv_b, dt_bias, a_log, d_skip_x, ssm_norm_w)


def _attn_kernel(*refs, c0):
    q_ref, g_ref = refs[0], refs[1]
    k_refs = refs[2:2 + BAND_CHUNKS]
    v_refs = refs[2 + BAND_CHUNKS:2 + 2 * BAND_CHUNKS]
    bias_ref = refs[2 + 2 * BAND_CHUNKS]
    o_ref = refs[3 + 2 * BAND_CHUNKS]
    kb_ref, vb_ref = refs[4 + 2 * BAND_CHUNKS:]
    c = pl.program_id(1)

    for i in range(BAND_CHUNKS):
        kb_ref[i * CHUNK:(i + 1) * CHUNK, :] = k_refs[i][...].astype(BF16)
        vb_ref[i * CHUNK:(i + 1) * CHUNK, :] = v_refs[i][...].astype(BF16)

    kpos = lax.broadcasted_iota(jnp.int32, (CHUNK, BAND), 1)
    valid = kpos >= (LEFT_CHUNKS - c - c0) * CHUNK
    lo = lax.broadcasted_iota(jnp.int32, (CHUNK, 2 * ATT_HEAD_DIM), 1) < ATT_HEAD_DIM

    for hp in range(ATT_HEADS // 2):
        sl = slice(hp * 128, (hp + 1) * 128)
        qp = q_ref[:, sl]
        kp = kb_ref[:, sl]
        vp = vb_ref[:, sl]
        zero = jnp.zeros_like(qp)
        outs = []
        for j in range(2):
            qm = jnp.where(lo, qp, zero) if j == 0 else jnp.where(lo, zero, qp)
            s = lax.dot_general(qm, kp, (((1,), (1,)), ((), ())),
                                preferred_element_type=F32)
            s = s + bias_ref[2 * hp + j]
            s = jnp.where(valid, s, NEG_INF)
            mx = jnp.max(s, axis=-1, keepdims=True)
            p = jnp.exp(s - mx)
            den = jnp.sum(p, axis=-1, keepdims=True)
            o = jnp.dot(p.astype(BF16), vp, preferred_element_type=F32)
            outs.append(o / den)
        o_pair = jnp.where(lo, outs[0], outs[1])
        gate = g_ref[:, sl].astype(F32)
        o_ref[:, sl] = (o_pair * (gate * _sigmoid(gate))).astype(BF16)


def _attn(q_arr, q_col, g_arr, g_col, k_arr, k_col, v_arr, v_col, bias, nb, nq, nkv, c0):
    def kv_spec(i, col):
        def idx(b, c):
            return (b * nkv + jnp.maximum(c + c0 - LEFT_CHUNKS + i, 0), col)
        return pl.BlockSpec((CHUNK, D_ATT), idx)

    qrow = lambda b, c: b * nq + c
    in_specs = [
        pl.BlockSpec((CHUNK, D_ATT), lambda b, c: (qrow(b, c), q_col)),
        pl.BlockSpec((CHUNK, D_ATT), lambda b, c: (qrow(b, c), g_col)),
    ]
    in_specs += [kv_spec(i, k_col) for i in range(BAND_CHUNKS)]
    in_specs += [kv_spec(i, v_col) for i in range(BAND_CHUNKS)]
    in_specs += [pl.BlockSpec((ATT_HEADS, CHUNK, BAND), lambda b, c: (0, 0, 0))]
    return pl.pallas_call(
        functools.partial(_attn_kernel, c0=c0),
        grid=(nb, nq),
        in_specs=in_specs,
        out_specs=pl.BlockSpec((CHUNK, D_ATT), lambda b, c: (qrow(b, c), 0)),
        out_shape=jax.ShapeDtypeStruct((nb * nq * CHUNK, D_ATT), BF16),
        scratch_shapes=[pltpu.VMEM((BAND, D_ATT), BF16), pltpu.VMEM((BAND, D_ATT), BF16)],
        compiler_params=pltpu.CompilerParams(
            dimension_semantics=("arbitrary", "arbitrary"),
            vmem_limit_bytes=VMEM_LIMIT),
        name="attn",
    )(q_arr, g_arr, *([k_arr] * BAND_CHUNKS), *([v_arr] * BAND_CHUNKS), bias)


OUT_TM = 512


def _out_kernel(yz_ref, og_ref, gs_ref, ga_ref, x_ref, wos_ref, woa_ref, wo_ref, fnw_ref, y_ref):
    y_ssm = jnp.dot(yz_ref[...], wos_ref[...], preferred_element_type=F32)
    y_att = jnp.dot(og_ref[...], woa_ref[...], preferred_element_type=F32)
    merged = (_sigmoid(gs_ref[...].astype(F32)) * y_ssm
              + _sigmoid(ga_ref[...].astype(F32)) * y_att)
    h = x_ref[...] + jnp.dot(merged.astype(BF16), wo_ref[...], preferred_element_type=F32)
    ms = jnp.mean(h * h, axis=-1, keepdims=True)
    y_ref[...] = (h * lax.rsqrt(ms + NORM_EPS)) * fnw_ref[...]


def _out(yz, og, p, x2d, w_out_ssm, w_out_att, w_o, final_norm_w):
    n = x2d.shape[0]
    tm = min(OUT_TM, n)
    const = lambda i: (0, 0)
    return pl.pallas_call(
        _out_kernel,
        grid=(n // tm,),
        in_specs=[
            pl.BlockSpec((tm, D_INNER), lambda i: (i, 0)),
            pl.BlockSpec((tm, D_ATT), lambda i: (i, 0)),
            pl.BlockSpec((tm, D_MODEL), lambda i: (i, COL_GS // D_MODEL)),
            pl.BlockSpec((tm, D_MODEL), lambda i: (i, COL_GA // D_MODEL)),
            pl.BlockSpec((tm, D_MODEL), lambda i: (i, 0)),
            pl.BlockSpec((D_INNER, D_MODEL), const),
            pl.BlockSpec((D_ATT, D_MODEL), const),
            pl.BlockSpec((D_MODEL, D_MODEL), const),
            pl.BlockSpec((1, D_MODEL), const),
        ],
        out_specs=pl.BlockSpec((tm, D_MODEL), lambda i: (i, 0)),
        out_shape=jax.ShapeDtypeStruct((n, D_MODEL), F32),
        compiler_params=pltpu.CompilerParams(
            dimension_semantics=("arbitrary",),
            vmem_limit_bytes=VMEM_LIMIT),
        name="out",
    )(yz, og, p, p, x2d, w_out_ssm, w_out_att, w_o, final_norm_w)


def _rel_bias_table(rel_bias):
    d = (BAND - CHUNK) + jnp.arange(CHUNK)[:, None] - jnp.arange(BAND)[None, :]
    idx = jnp.clip(d, -REL_CLIP, REL_CLIP) + REL_CLIP
    return rel_bias[:, idx].astype(F32)


def kernel(x_prompt, x_sample, state_ssm, state_conv, cache_k, cache_v, norm_w, w_in, conv_w,
           conv_b, dt_bias, a_log, d_skip, ssm_norm_w, w_out_ssm, rel_bias, w_out_att, w_o,
           final_norm_w):
    bp, lp, _ = x_prompt.shape
    bs, ls, _ = x_sample.shape
    ncp = lp // CHUNK
    assert ls == CHUNK and cache_k.shape[2] == LEFT_CHUNKS * CHUNK

    w = w_in[0]
    o = 0
    parts = {}
    for name, size in (("z", D_INNER), ("xbc", CONV_DIM), ("dt", SSM_HEADS), ("q", D_ATT),
                       ("k", D_ATT), ("v", D_ATT), ("g", D_ATT), ("gs", D_MODEL), ("ga", D_MODEL)):
        parts[name] = w[:, o:o + size]
        o += size
    w_main = jnp.concatenate(
        [parts["z"], parts["xbc"], parts["q"] * (ATT_HEAD_DIM ** -0.5), parts["k"], parts["v"],
         parts["g"], parts["gs"], parts["ga"]], axis=1).astype(BF16)
    w_dt = jnp.pad(parts["dt"], ((0, 0), (0, DT_PAD - SSM_HEADS))).astype(BF16)
    nw = norm_w[0].reshape(1, D_MODEL)
    cw = conv_w[0]
    cb = conv_b[0].reshape(1, CONV_DIM)
    dtb = jnp.pad(dt_bias[0], (0, DT_PAD - SSM_HEADS)).reshape(1, DT_PAD)
    alog = jnp.pad(a_log[0], (0, DT_PAD - SSM_HEADS)).reshape(1, DT_PAD)
    dskip_x = jnp.repeat(d_skip[0], SSM_HEAD_DIM).reshape(1, D_INNER)
    snw = ssm_norm_w[0].reshape(1, D_INNER)
    wos = w_out_ssm[0].astype(BF16)
    woa = w_out_att[0].astype(BF16)
    wo = w_o[0].astype(BF16)
    fnw = final_norm_w.reshape(1, D_MODEL)
    bias = _rel_bias_table(rel_bias[0])

    ssd_params = (cw, cb, dtb, alog, dskip_x, snw)
    out_params = (wos, woa, wo, fnw)

    xp = x_prompt.reshape(bp * lp, D_MODEL)
    p_p, dt_p = _proj(xp, nw, w_main, w_dt)
    h0_p = jnp.zeros((bp, D_INNER, D_STATE), F32)
    conv0_p = jnp.zeros((bp, 8, CONV_DIM), F32)
    yz_p, ssm_p, tail_p = _ssd(p_p, dt_p, h0_p, conv0_p, *ssd_params, nb=bp, nc=ncp)
    og_p = _attn(p_p, COL_Q // D_ATT, p_p, COL_G // D_ATT, p_p, COL_K // D_ATT,
                 p_p, COL_V // D_ATT, bias, nb=bp, nq=ncp, nkv=ncp, c0=0)
    y_p = _out(yz_p, og_p, p_p, xp, *out_params).reshape(bp, lp, D_MODEL)
    keep = LEFT_CHUNKS * CHUNK
    p3 = p_p.reshape(bp, lp, P_COLS)
    k_p = p3[:, lp - keep:, COL_K:COL_K + D_ATT].astype(F32).reshape(1, bp, keep, ATT_HEADS, ATT_HEAD_DIM)
    v_p = p3[:, lp - keep:, COL_V:COL_V + D_ATT].astype(F32).reshape(1, bp, keep, ATT_HEADS, ATT_HEAD_DIM)

    xs = x_sample.reshape(bs * ls, D_MODEL)
    p_s, dt_s = _proj(xs, nw, w_main, w_dt)
    h0_s = state_ssm[0].reshape(bs, D_INNER, D_STATE)
    conv0_s = jnp.pad(state_conv[0], ((0, 0), (8 - (CONV_W - 1), 0), (0, 0)))
    yz_s, ssm_s, tail_s = _ssd(p_s, dt_s, h0_s, conv0_s, *ssd_params, nb=bs, nc=1)
    ps3 = p_s.reshape(bs, ls, P_COLS)
    kk = jnp.concatenate([cache_k[0].reshape(bs, keep, D_ATT),
                          ps3[:, :, COL_K:COL_K + D_ATT].astype(F32)], axis=1)
    vv = jnp.concatenate([cache_v[0].reshape(bs, keep, D_ATT),
                          ps3[:, :, COL_V:COL_V + D_ATT].astype(F32)], axis=1)
    og_s = _attn(p_s, COL_Q // D_ATT, p_s, COL_G // D_ATT,
                 kk.reshape(bs * BAND, D_ATT), 0, vv.reshape(bs * BAND, D_ATT), 0,
                 bias, nb=bs, nq=1, nkv=BAND_CHUNKS, c0=LEFT_CHUNKS)
    y_s = _out(yz_s, og_s, p_s, xs, *out_params).reshape(bs, ls, D_MODEL)
    k_s = kk[:, ls:].reshape(1, bs, keep, ATT_HEADS, ATT_HEAD_DIM)
    v_s = vv[:, ls:].reshape(1, bs, keep, ATT_HEADS, ATT_HEAD_DIM)

    def states(ssm, tail, nb):
        return (ssm.reshape(1, nb, SSM_HEADS, SSM_HEAD_DIM, D_STATE),
                tail[:, 8 - (CONV_W - 1):, :].reshape(1, nb, CONV_W - 1, CONV_DIM))

    ssm_p5, conv_p4 = states(ssm_p, tail_p, bp)
    ssm_s5, conv_s4 = states(ssm_s, tail_s, bs)
    return (y_p, y_s, ssm_p5, conv_p4, k_p, v_p, ssm_s5, conv_s4, k_s, v_s)
```

```python
import functools

import jax
import jax.numpy as jnp
from jax import lax
from jax.experimental import pallas as pl
from jax.experimental.pallas import tpu as pltpu

D_MODEL = 1024
CHUNK = 64
D_INNER = 2048
SSM_HEAD_DIM = 64
SSM_HEADS = 32
N_GROUPS = 4
D_STATE = 128
GROUP_W = D_INNER // N_GROUPS
BC_W = N_GROUPS * D_STATE
CONV_W = 4
CONV_DIM = D_INNER + 2 * BC_W
ATT_HEADS = 16
ATT_HEAD_DIM = 64
D_ATT = 1024
LEFT_CHUNKS = 8
BAND_CHUNKS = LEFT_CHUNKS + 1
BAND = BAND_CHUNKS * CHUNK
REL_CLIP = 256
NORM_EPS = 1e-5
NEG_INF = -1e30
LOG2E = 1.4426950408889634

COL_Z = 0
COL_X = 2048
COL_B = 4096
COL_C = 4608
COL_Q = 5120
COL_K = 6144
COL_V = 7168
COL_G = 8192
COL_GS = 9216
COL_GA = 10240
P_COLS = 11264
DT_PAD = 128

VMEM_LIMIT = 56 * 1024 * 1024

F32 = jnp.float32
BF16 = jnp.bfloat16


def _sigmoid(x):
    return 1.0 / (1.0 + jnp.exp(-x))


def _softplus(x):
    return jnp.maximum(x, 0.0) + jnp.log(1.0 + jnp.exp(-jnp.abs(x)))


PROJ_TM = 1024
PROJ_TN = 1024


def _proj_kernel(x_ref, nw_ref, w_ref, wdt_ref, p_ref, dt_ref, xn_ref):
    @pl.when(pl.program_id(1) == 0)
    def _():
        x = x_ref[...]
        ms = jnp.mean(x * x, axis=-1, keepdims=True)
        xn = (x * lax.rsqrt(ms + NORM_EPS)) * nw_ref[...]
        xn_ref[...] = xn.astype(BF16)
        dt_ref[...] = jnp.dot(xn_ref[...], wdt_ref[...], preferred_element_type=F32)

    p_ref[...] = jnp.dot(xn_ref[...], w_ref[...], preferred_element_type=F32).astype(BF16)


def _proj(x2d, norm_w, w_main, w_dt):
    n = x2d.shape[0]
    tm = min(PROJ_TM, n)
    return pl.pallas_call(
        _proj_kernel,
        grid=(n // tm, P_COLS // PROJ_TN),
        in_specs=[
            pl.BlockSpec((tm, D_MODEL), lambda i, j: (i, 0)),
            pl.BlockSpec((1, D_MODEL), lambda i, j: (0, 0)),
            pl.BlockSpec((D_MODEL, PROJ_TN), lambda i, j: (0, j)),
            pl.BlockSpec((D_MODEL, DT_PAD), lambda i, j: (0, 0)),
        ],
        out_specs=[
            pl.BlockSpec((tm, PROJ_TN), lambda i, j: (i, j)),
            pl.BlockSpec((tm, DT_PAD), lambda i, j: (i, 0)),
        ],
        out_shape=[
            jax.ShapeDtypeStruct((n, P_COLS), BF16),
            jax.ShapeDtypeStruct((n, DT_PAD), F32),
        ],
        scratch_shapes=[pltpu.VMEM((tm, D_MODEL), BF16)],
        compiler_params=pltpu.CompilerParams(
            dimension_semantics=("arbitrary", "arbitrary"),
            vmem_limit_bytes=VMEM_LIMIT),
        name="proj",
    )(x2d, norm_w, w_main, w_dt)


def _conv_silu(u_bf16, tail, w, b):
    u = u_bf16.astype(F32)
    row8 = lax.broadcasted_iota(jnp.int32, (8, 1), 0)
    acc = b + w[CONV_W - 1:CONV_W, :] * u
    for s in range(1, CONV_W):
        us = pltpu.roll(u, s, axis=0)
        ts = pltpu.roll(tail, s, axis=0)
        head = jnp.where(row8 < s, ts, us[0:8, :])
        us = jnp.concatenate([head, us[8:, :]], axis=0)
        acc = acc + w[CONV_W - 1 - s:CONV_W - s, :] * us
    return acc * _sigmoid(acc), u[CHUNK - 8:, :]


def _ssd_kernel(z_ref, x_ref, b_ref, c_ref, dtr_ref, h0_ref, conv0_ref,
                cw_ref, cb_ref, dtb_ref, alog_ref, dskip_ref, nw_ref,
                yz_ref, hout_ref, tailout_ref,
                ht_ref, tail_ref, xw_ref, y_ref, ea_ref):
    c = pl.program_id(1)
    n_chunks = pl.num_programs(1)

    @pl.when(c == 0)
    def _():
        for g in range(N_GROUPS):
            sl = slice(g * GROUP_W, (g + 1) * GROUP_W)
            ht_ref[:, sl] = h0_ref[0, sl, :].T
        tail_ref[...] = conv0_ref[0]

    xs, tail_x = _conv_silu(x_ref[...], tail_ref[:, 0:D_INNER],
                            cw_ref[:, 0:D_INNER], cb_ref[:, 0:D_INNER])
    bm, tail_b = _conv_silu(b_ref[...], tail_ref[:, D_INNER:D_INNER + BC_W],
                            cw_ref[:, D_INNER:D_INNER + BC_W], cb_ref[:, D_INNER:D_INNER + BC_W])
    cm, tail_c = _conv_silu(c_ref[...], tail_ref[:, D_INNER + BC_W:],
                            cw_ref[:, D_INNER + BC_W:], cb_ref[:, D_INNER + BC_W:])
    tail_ref[:, 0:D_INNER] = tail_x
    tail_ref[:, D_INNER:D_INNER + BC_W] = tail_b
    tail_ref[:, D_INNER + BC_W:] = tail_c
    bm_b = bm.astype(BF16)
    cm_b = cm.astype(BF16)

    dt = _softplus(dtr_ref[...] + dtb_ref[...])
    adt = dt * (-jnp.exp(alog_ref[...]))
    li = lax.broadcasted_iota(jnp.int32, (CHUNK, CHUNK), 0)
    si = lax.broadcasted_iota(jnp.int32, (CHUNK, CHUNK), 1)
    tri = (li >= si).astype(F32)
    acs = jnp.dot(tri, adt, precision=lax.Precision.HIGHEST,
                  preferred_element_type=F32)
    s2 = lax.broadcasted_iota(jnp.int32, (CHUNK, 2 * CHUNK), 0)
    l2 = lax.broadcasted_iota(jnp.int32, (CHUNK, 2 * CHUNK), 1) % CHUNK
    tri_t2 = (l2 >= s2).astype(F32)
    acs_t2 = lax.dot_general(adt, tri_t2, (((0,), (0,)), ((), ())),
                             precision=lax.Precision.HIGHEST,
                             preferred_element_type=F32)

    lane = lax.broadcasted_iota(jnp.int32, (CHUNK, 2 * CHUNK), 1)
    lo = lane < CHUNK
    lo_row = lo[0:1, :]
    tri2 = (lax.broadcasted_iota(jnp.int32, (CHUNK, 2 * CHUNK), 0) >= (lane % CHUNK))

    yoff = []
    cb2 = []
    for g in range(N_GROUPS):
        sl = slice(g * GROUP_W, (g + 1) * GROUP_W)
        ns = slice(g * D_STATE, (g + 1) * D_STATE)
        yoff.append(jnp.dot(cm_b[:, ns], ht_ref[:, sl].astype(BF16),
                            preferred_element_type=F32))
        b2 = jnp.concatenate([bm_b[:, ns], bm_b[:, ns]], axis=0)
        cb2.append(lax.dot_general(cm_b[:, ns], b2, (((1,), (1,)), ((), ())),
                                   preferred_element_type=F32))

    for j in range(SSM_HEADS // 2):
        g = j // (SSM_HEADS // 2 // N_GROUPS)
        r0, r1 = 2 * j, 2 * j + 1
        sl = slice(j * 128, (j + 1) * 128)
        gsl = slice((j % 4) * 128, (j % 4 + 1) * 128)
        u_col = jnp.where(lo, acs[:, r0:r0 + 1], acs[:, r1:r1 + 1])
        v_row = jnp.where(lo_row, acs_t2[r0:r0 + 1, :], acs_t2[r1:r1 + 1, :])
        dtp = jnp.where(lo, dt[:, r0:r0 + 1], dt[:, r1:r1 + 1])
        decay = jnp.exp(jnp.where(tri2, u_col - v_row, -jnp.inf))
        m = (cb2[g] * decay).astype(BF16)
        xs_p = xs[:, sl]
        xdt = xs_p * dtp
        xdt_b = xdt.astype(BF16)
        zero = jnp.zeros_like(xdt_b)
        rhs = jnp.concatenate([jnp.where(lo, xdt_b, zero), jnp.where(lo, zero, xdt_b)], axis=0)
        yd = jnp.dot(m, rhs, preferred_element_type=F32)
        e_u = jnp.exp(u_col)
        y = yd + yoff[g][:, gsl] * e_u + dskip_ref[:, sl] * xs_p
        to_end = jnp.exp(u_col[CHUNK - 1:CHUNK, :] - u_col)
        xw_ref[:, sl] = (xdt * to_end).astype(BF16)
        ea_ref[:, sl] = e_u[CHUNK - 8:CHUNK, :]
        zf = z_ref[:, sl].astype(F32)
        y_ref[:, sl] = y * (zf * _sigmoid(zf))

    for g in range(N_GROUPS):
        sl = slice(g * GROUP_W, (g + 1) * GROUP_W)
        ns = slice(g * D_STATE, (g + 1) * D_STATE)
        yg = y_ref[:, sl]
        ms = jnp.mean(yg * yg, axis=-1, keepdims=True)
        yz_ref[:, sl] = ((yg * lax.rsqrt(ms + NORM_EPS)) * nw_ref[:, sl]).astype(BF16)
        upd = lax.dot_general(bm_b[:, ns], xw_ref[:, sl], (((0,), (0,)), ((), ())),
                              preferred_element_type=F32)
        ht_ref[:, sl] = ht_ref[:, sl] * ea_ref[7:8, sl] + upd

    @pl.when(c == n_chunks - 1)
    def _():
        for g in range(N_GROUPS):
            sl = slice(g * GROUP_W, (g + 1) * GROUP_W)
            hout_ref[0, sl, :] = ht_ref[:, sl].T
        tailout_ref[0] = tail_ref[...]


def _ssd(p, dtraw, h0, conv0, conv_w, conv_b, dt_bias, a_log, d_skip_x, ssm_norm_w, nb, nc):
    n = nb * nc * CHUNK
    row = lambda b, c: b * nc + c
    const = lambda b, c: (0, 0)
    return pl.pallas_call(
        _ssd_kernel,
        grid=(nb, nc),
        in_specs=[
            pl.BlockSpec((CHUNK, D_INNER), lambda b, c: (row(b, c), COL_Z // D_INNER)),
            pl.BlockSpec((CHUNK, D_INNER), lambda b, c: (row(b, c), COL_X // D_INNER)),
            pl.BlockSpec((CHUNK, BC_W), lambda b, c: (row(b, c), COL_B // BC_W)),
            pl.BlockSpec((CHUNK, BC_W), lambda b, c: (row(b, c), COL_C // BC_W)),
            pl.BlockSpec((CHUNK, DT_PAD), lambda b, c: (row(b, c), 0)),
            pl.BlockSpec((1, D_INNER, D_STATE), lambda b, c: (b, 0, 0)),
            pl.BlockSpec((1, 8, CONV_DIM), lambda b, c: (b, 0, 0)),
            pl.BlockSpec((CONV_W, CONV_DIM), const),
            pl.BlockSpec((1, CONV_DIM), const),
            pl.BlockSpec((1, DT_PAD), const),
            pl.BlockSpec((1, DT_PAD), const),
            pl.BlockSpec((1, D_INNER), const),
            pl.BlockSpec((1, D_INNER), const),
        ],
        out_specs=[
            pl.BlockSpec((CHUNK, D_INNER), lambda b, c: (row(b, c), 0)),
            pl.BlockSpec((1, D_INNER, D_STATE), lambda b, c: (b, 0, 0)),
            pl.BlockSpec((1, 8, CONV_DIM), lambda b, c: (b, 0, 0)),
        ],
        out_shape=[
            jax.ShapeDtypeStruct((n, D_INNER), BF16),
            jax.ShapeDtypeStruct((nb, D_INNER, D_STATE), F32),
            jax.ShapeDtypeStruct((nb, 8, CONV_DIM), F32),
        ],
        scratch_shapes=[
            pltpu.VMEM((D_STATE, D_INNER), F32),
            pltpu.VMEM((8, CONV_DIM), F32),
            pltpu.VMEM((CHUNK, D_INNER), BF16),
            pltpu.VMEM((CHUNK, D_INNER), F32),
            pltpu.VMEM((8, D_INNER), F32),
        ],
        compiler_params=pltpu.CompilerParams(
            dimension_semantics=("arbitrary", "arbitrary"),
            vmem_limit_bytes=VMEM_LIMIT),
        name="ssd",
    )(p, p, p, p, dtraw, h0, conv0, conv_w, conv_b, dt_bias, a_log, d_skip_x, ssm_norm_w)


ATT_TQ = 256
ATT_TILES = 3
ATT_KEYS = ATT_TILES * ATT_TQ
BIAS_ROW = 1024


def _bias_kernel(r_ref, o_ref):
    x = jnp.broadcast_to(r_ref[0], (ATT_TQ, BIAS_ROW))
    tab = pltpu.roll(x, BIAS_ROW - ATT_TQ, axis=1, stride=1, stride_axis=0)[:, :ATT_KEYS]
    qc = lax.broadcasted_iota(jnp.int32, (ATT_TQ, ATT_KEYS), 0) // CHUNK
    kc = lax.broadcasted_iota(jnp.int32, (ATT_TQ, ATT_KEYS), 1) // CHUNK
    ok = (kc >= qc) & (kc <= qc + LEFT_CHUNKS)
    o_ref[0] = jnp.where(ok, tab * LOG2E, NEG_INF)


def _bias_table(rel_row):
    return pl.pallas_call(
        _bias_kernel,
        grid=(ATT_HEADS,),
        in_specs=[pl.BlockSpec((1, 1, BIAS_ROW), lambda h: (h, 0, 0))],
        out_specs=pl.BlockSpec((1, ATT_TQ, ATT_KEYS), lambda h: (h, 0, 0)),
        out_shape=jax.ShapeDtypeStruct((ATT_HEADS, ATT_TQ, ATT_KEYS), F32),
        name="bias_table",
    )(rel_row)


def _softmax_pv(s_tiles, v_tiles):
    mx = functools.reduce(jnp.maximum, [jnp.max(s, axis=-1, keepdims=True) for s in s_tiles])
    p_tiles = [jnp.exp2(s - mx) for s in s_tiles]
    den = functools.reduce(jnp.add, [jnp.sum(p, axis=-1, keepdims=True) for p in p_tiles])
    o = functools.reduce(jnp.add, [jnp.dot(p.astype(BF16), v, preferred_element_type=F32)
                                   for p, v in zip(p_tiles, v_tiles)])
    return o / den


def _attn_prompt_kernel(q_ref, g_ref, k0_ref, k1_ref, k2_ref, v0_ref, v1_ref, v2_ref,
                        bias_ref, o_ref):
    t = pl.program_id(1)
    k_refs = (k0_ref, k1_ref, k2_ref)
    v_refs = (v0_ref, v1_ref, v2_ref)

    def body(mask_start):
        lo = lax.broadcasted_iota(jnp.int32, (ATT_TQ, 2 * ATT_HEAD_DIM), 1) < ATT_HEAD_DIM
        for hp in range(ATT_HEADS // 2):
            sl = slice(hp * 128, (hp + 1) * 128)
            qp = q_ref[:, sl]
            zero = jnp.zeros_like(qp)
            outs = []
            for j in range(2):
                h = 2 * hp + j
                qm = jnp.where(lo, qp, zero) if j == 0 else jnp.where(lo, zero, qp)
                s_tiles = []
                for n in range(ATT_TILES):
                    s = lax.dot_general(qm, k_refs[n][:, sl], (((1,), (1,)), ((), ())),
                                        preferred_element_type=F32)
                    s = s + bias_ref[h, :, n * ATT_TQ:(n + 1) * ATT_TQ]
                    if mask_start and n < ATT_TILES - 1:
                        s = jnp.where(t + n >= ATT_TILES - 1, s, NEG_INF)
                    s_tiles.append(s)
                outs.append(_softmax_pv(s_tiles, [v_refs[n][:, sl] for n in range(ATT_TILES)]))
            o_pair = jnp.where(lo, outs[0], outs[1])
            gate = g_ref[:, sl].astype(F32)
            o_ref[:, sl] = (o_pair * (gate * _sigmoid(gate))).astype(BF16)

    pl.when(t < ATT_TILES - 1)(functools.partial(body, True))
    pl.when(t >= ATT_TILES - 1)(functools.partial(body, False))


def _attn_prompt(p, bias, nb, seq):
    nt = seq // ATT_TQ

    def kv_spec(n, col):
        return pl.BlockSpec(
            (ATT_TQ, D_ATT),
            lambda b, t: (b * nt + jnp.maximum(t - (ATT_TILES - 1) + n, 0), col))

    in_specs = [
        pl.BlockSpec((ATT_TQ, D_ATT), lambda b, t: (b * nt + t, COL_Q // D_ATT)),
        pl.BlockSpec((ATT_TQ, D_ATT), lambda b, t: (b * nt + t, COL_G // D_ATT)),
    ]
    in_specs += [kv_spec(n, COL_K // D_ATT) for n in range(ATT_TILES)]
    in_specs += [kv_spec(n, COL_V // D_ATT) for n in range(ATT_TILES)]
    in_specs += [pl.BlockSpec((ATT_HEADS, ATT_TQ, ATT_KEYS), lambda b, t: (0, 0, 0),
                              pipeline_mode=pl.Buffered(1))]
    return pl.pallas_call(
        _attn_prompt_kernel,
        grid=(nb, nt),
        in_specs=in_specs,
        out_specs=pl.BlockSpec((ATT_TQ, D_ATT), lambda b, t: (b * nt + t, 0)),
        out_shape=jax.ShapeDtypeStruct((nb * seq, D_ATT), BF16),
        compiler_params=pltpu.CompilerParams(
            dimension_semantics=("arbitrary", "arbitrary"),
            vmem_limit_bytes=VMEM_LIMIT),
        name="attn_prompt",
    )(p, p, p, p, p, p, p, p, bias)


def _attn_step_kernel(q_ref, g_ref, kn_ref, vn_ref, kc_ref, vc_ref, bias_ref,
                      o_ref, ks_ref, vs_ref):
    keep = LEFT_CHUNKS * CHUNK
    ks_ref[0, 0:keep - CHUNK, :] = kc_ref[0, CHUNK:keep, :]
    ks_ref[0, keep - CHUNK:keep, :] = kn_ref[...].astype(F32)
    vs_ref[0, 0:keep - CHUNK, :] = vc_ref[0, CHUNK:keep, :]
    vs_ref[0, keep - CHUNK:keep, :] = vn_ref[...].astype(F32)
    lo = lax.broadcasted_iota(jnp.int32, (CHUNK, 2 * ATT_HEAD_DIM), 1) < ATT_HEAD_DIM

    for hp in range(ATT_HEADS // 2):
        sl = slice(hp * 128, (hp + 1) * 128)
        qp = q_ref[:, sl]
        k_tiles = [kc_ref[0, :, sl].astype(BF16), kn_ref[:, sl]]
        v_tiles = [vc_ref[0, :, sl].astype(BF16), vn_ref[:, sl]]
        zero = jnp.zeros_like(qp)
        outs = []
        for j in range(2):
            h = 2 * hp + j
            qm = jnp.where(lo, qp, zero) if j == 0 else jnp.where(lo, zero, qp)
            s_tiles = []
            for n, (k0, k1) in enumerate(((0, keep), (keep, BAND))):
                s = lax.dot_general(qm, k_tiles[n], (((1,), (1,)), ((), ())),
                                    preferred_element_type=F32)
                s_tiles.append(s + bias_ref[h, 0:CHUNK, k0:k1])
            outs.append(_softmax_pv(s_tiles, v_tiles))
        o_pair = jnp.where(lo, outs[0], outs[1])
        gate = g_ref[:, sl].astype(F32)
        o_ref[:, sl] = (o_pair * (gate * _sigmoid(gate))).astype(BF16)


def _attn_step(p, cache_k, cache_v, bias, nb):
    keep = LEFT_CHUNKS * CHUNK
    new = lambda col: pl.BlockSpec((CHUNK, D_ATT), lambda b: (b, col))
    cache = pl.BlockSpec((1, keep, D_ATT), lambda b: (b, 0, 0))
    return pl.pallas_call(
        _attn_step_kernel,
        grid=(nb,),
        in_specs=[new(COL_Q // D_ATT), new(COL_G // D_ATT), new(COL_K // D_ATT),
                  new(COL_V // D_ATT), cache, cache,
                  pl.BlockSpec((ATT_HEADS, ATT_TQ, ATT_KEYS), lambda b: (0, 0, 0),
                               pipeline_mode=pl.Buffered(1))],
        out_specs=[pl.BlockSpec((CHUNK, D_ATT), lambda b: (b, 0)), cache, cache],
        out_shape=[jax.ShapeDtypeStruct((nb * CHUNK, D_ATT), BF16),
                   jax.ShapeDtypeStruct((nb, keep, D_ATT), F32),
                   jax.ShapeDtypeStruct((nb, keep, D_ATT), F32)],
        compiler_params=pltpu.CompilerParams(
            dimension_semantics=("arbitrary",),
            vmem_limit_bytes=VMEM_LIMIT),
        name="attn_step",
    )(p, p, p, p, cache_k, cache_v, bias)


OUT_TM = 512


def _out_kernel(yz_ref, og_ref, gs_ref, ga_ref, x_ref, wos_ref, woa_ref, wo_ref, fnw_ref, y_ref):
    y_ssm = jnp.dot(yz_ref[...], wos_ref[...], preferred_element_type=F32)
    y_att = jnp.dot(og_ref[...], woa_ref[...], preferred_element_type=F32)
    merged = (_sigmoid(gs_ref[...].astype(F32)) * y_ssm
              + _sigmoid(ga_ref[...].astype(F32)) * y_att)
    h = x_ref[...] + jnp.dot(merged.astype(BF16), wo_ref[...], preferred_element_type=F32)
    ms = jnp.mean(h * h, axis=-1, keepdims=True)
    y_ref[...] = (h * lax.rsqrt(ms + NORM_EPS)) * fnw_ref[...]


def _out(yz, og, p, x2d, w_out_ssm, w_out_att, w_o, final_norm_w):
    n = x2d.shape[0]
    tm = min(OUT_TM, n)
    const = lambda i: (0, 0)
    return pl.pallas_call(
        _out_kernel,
        grid=(n // tm,),
        in_specs=[
            pl.BlockSpec((tm, D_INNER), lambda i: (i, 0)),
            pl.BlockSpec((tm, D_ATT), lambda i: (i, 0)),
            pl.BlockSpec((tm, D_MODEL), lambda i: (i, COL_GS // D_MODEL)),
            pl.BlockSpec((tm, D_MODEL), lambda i: (i, COL_GA // D_MODEL)),
            pl.BlockSpec((tm, D_MODEL), lambda i: (i, 0)),
            pl.BlockSpec((D_INNER, D_MODEL), const),
            pl.BlockSpec((D_ATT, D_MODEL), const),
            pl.BlockSpec((D_MODEL, D_MODEL), const),
            pl.BlockSpec((1, D_MODEL), const),
        ],
        out_specs=pl.BlockSpec((tm, D_MODEL), lambda i: (i, 0)),
        out_shape=jax.ShapeDtypeStruct((n, D_MODEL), F32),
        compiler_params=pltpu.CompilerParams(
            dimension_semantics=("arbitrary",),
            vmem_limit_bytes=VMEM_LIMIT),
        name="out",
    )(yz, og, p, p, x2d, w_out_ssm, w_out_att, w_o, final_norm_w)


def _rel_bias_rows(rel_bias):
    d = ATT_TILES * ATT_TQ - jnp.arange(BIAS_ROW)
    idx = jnp.clip(d, -REL_CLIP, REL_CLIP) + REL_CLIP
    return rel_bias[:, idx].astype(F32).reshape(ATT_HEADS, 1, BIAS_ROW)


def kernel(x_prompt, x_sample, state_ssm, state_conv, cache_k, cache_v, norm_w, w_in, conv_w,
           conv_b, dt_bias, a_log, d_skip, ssm_norm_w, w_out_ssm, rel_bias, w_out_att, w_o,
           final_norm_w):
    bp, lp, _ = x_prompt.shape
    bs, ls, _ = x_sample.shape
    ncp = lp // CHUNK
    assert ls == CHUNK and cache_k.shape[2] == LEFT_CHUNKS * CHUNK

    w = w_in[0]
    o = 0
    parts = {}
    for name, size in (("z", D_INNER), ("xbc", CONV_DIM), ("dt", SSM_HEADS), ("q", D_ATT),
                       ("k", D_ATT), ("v", D_ATT), ("g", D_ATT), ("gs", D_MODEL), ("ga", D_MODEL)):
        parts[name] = w[:, o:o + size]
        o += size
    w_main = jnp.concatenate(
        [parts["z"], parts["xbc"], parts["q"] * (ATT_HEAD_DIM ** -0.5 * LOG2E), parts["k"], parts["v"],
         parts["g"], parts["gs"], parts["ga"]], axis=1).astype(BF16)
    w_dt = jnp.pad(parts["dt"], ((0, 0), (0, DT_PAD - SSM_HEADS))).astype(BF16)
    nw = norm_w[0].reshape(1, D_MODEL)
    cw = conv_w[0]
    cb = conv_b[0].reshape(1, CONV_DIM)
    dtb = jnp.pad(dt_bias[0], (0, DT_PAD - SSM_HEADS)).reshape(1, DT_PAD)
    alog = jnp.pad(a_log[0], (0, DT_PAD - SSM_HEADS)).reshape(1, DT_PAD)
    dskip_x = jnp.repeat(d_skip[0], SSM_HEAD_DIM).reshape(1, D_INNER)
    snw = ssm_norm_w[0].reshape(1, D_INNER)
    wos = w_out_ssm[0].astype(BF16)
    woa = w_out_att[0].astype(BF16)
    wo = w_o[0].astype(BF16)
    fnw = final_norm_w.reshape(1, D_MODEL)
    bias = _bias_table(_rel_bias_rows(rel_bias[0]))

    ssd_params = (cw, cb, dtb, alog, dskip_x, snw)
    out_params = (wos, woa, wo, fnw)

    xp = x_prompt.reshape(bp * lp, D_MODEL)
    p_p, dt_p = _proj(xp, nw, w_main, w_dt)
    h0_p = jnp.zeros((bp, D_INNER, D_STATE), F32)
    conv0_p = jnp.zeros((bp, 8, CONV_DIM), F32)
    yz_p, ssm_p, tail_p = _ssd(p_p, dt_p, h0_p, conv0_p, *ssd_params, nb=bp, nc=ncp)
    og_p = _attn_prompt(p_p, bias, nb=bp, seq=lp)
    y_p = _out(yz_p, og_p, p_p, xp, *out_params).reshape(bp, lp, D_MODEL)
    keep = LEFT_CHUNKS * CHUNK
    p3 = p_p.reshape(bp, lp, P_COLS)
    k_p = p3[:, lp - keep:, COL_K:COL_K + D_ATT].astype(F32).reshape(1, bp, keep, ATT_HEADS, ATT_HEAD_DIM)
    v_p = p3[:, lp - keep:, COL_V:COL_V + D_ATT].astype(F32).reshape(1, bp, keep, ATT_HEADS, ATT_HEAD_DIM)

    xs = x_sample.reshape(bs * ls, D_MODEL)
    p_s, dt_s = _proj(xs, nw, w_main, w_dt)
    h0_s = state_ssm[0].reshape(bs, D_INNER, D_STATE)
    conv0_s = jnp.pad(state_conv[0], ((0, 0), (8 - (CONV_W - 1), 0), (0, 0)))
    yz_s, ssm_s, tail_s = _ssd(p_s, dt_s, h0_s, conv0_s, *ssd_params, nb=bs, nc=1)
    og_s, k_s, v_s = _attn_step(p_s, cache_k[0].reshape(bs, keep, D_ATT),
                                cache_v[0].reshape(bs, keep, D_ATT), bias, nb=bs)
    y_s = _out(yz_s, og_s, p_s, xs, *out_params).reshape(bs, ls, D_MODEL)
    k_s = k_s.reshape(1, bs, keep, ATT_HEADS, ATT_HEAD_DIM)
    v_s = v_s.reshape(1, bs, keep, ATT_HEADS, ATT_HEAD_DIM)

    def states(ssm, tail, nb):
        return (ssm.reshape(1, nb, SSM_HEADS, SSM_HEAD_DIM, D_STATE),
                tail[:, 8 - (CONV_W - 1):, :].reshape(1, nb, CONV_W - 1, CONV_DIM))

    ssm_p5, conv_p4 = states(ssm_p, tail_p, bp)
    ssm_s5, conv_s4 = states(ssm_s, tail_s, bs)
    return (y_p, y_s, ssm_p5, conv_p4, k_p, v_p, ssm_s5, conv_s4, k_s, v_s)
```

```python
import functools

import jax
import jax.numpy as jnp
from jax import lax
from jax.experimental import pallas as pl
from jax.experimental.pallas import tpu as pltpu

D_MODEL = 1024
CHUNK = 64
D_INNER = 2048
SSM_HEAD_DIM = 64
SSM_HEADS = 32
N_GROUPS = 4
D_STATE = 128
GROUP_W = D_INNER // N_GROUPS
BC_W = N_GROUPS * D_STATE
CONV_W = 4
CONV_DIM = D_INNER + 2 * BC_W
ATT_HEADS = 16
ATT_HEAD_DIM = 64
D_ATT = 1024
LEFT_CHUNKS = 8
BAND_CHUNKS = LEFT_CHUNKS + 1
BAND = BAND_CHUNKS * CHUNK
REL_CLIP = 256
NORM_EPS = 1e-5
NEG_INF = -1e30
LOG2E = 1.4426950408889634

COL_Z = 0
COL_X = 2048
COL_B = 4096
COL_C = 4608
COL_Q = 5120
COL_K = 6144
COL_V = 7168
COL_G = 8192
COL_GS = 9216
COL_GA = 10240
P_COLS = 11264
DT_PAD = 128

VMEM_LIMIT = 56 * 1024 * 1024

F32 = jnp.float32
BF16 = jnp.bfloat16


def _sigmoid(x):
    return 0.5 * jnp.tanh(0.5 * x) + 0.5


def _silu(x):
    h = 0.5 * x
    return h * jnp.tanh(h) + h


def _softplus(x):
    return jnp.maximum(x, 0.0) + jnp.log(1.0 + jnp.exp(-jnp.abs(x)))


PROJ_TM = 1024
PROJ_TN = 2816


def _proj_kernel(x_ref, nw_ref, w_ref, wdt_ref, p_ref, dt_ref, xn_ref):
    @pl.when(pl.program_id(1) == 0)
    def _():
        x = x_ref[...]
        ms = jnp.mean(x * x, axis=-1, keepdims=True)
        xn = (x * lax.rsqrt(ms + NORM_EPS)) * nw_ref[...]
        xn_ref[...] = xn.astype(BF16)
        dt_ref[...] = jnp.dot(xn_ref[...], wdt_ref[...], preferred_element_type=F32)

    p_ref[...] = jnp.dot(xn_ref[...], w_ref[...], preferred_element_type=F32).astype(BF16)


def _proj(x2d, norm_w, w_main, w_dt):
    n = x2d.shape[0]
    tm = min(PROJ_TM, n)
    return pl.pallas_call(
        _proj_kernel,
        grid=(n // tm, P_COLS // PROJ_TN),
        in_specs=[
            pl.BlockSpec((tm, D_MODEL), lambda i, j: (i, 0)),
            pl.BlockSpec((1, D_MODEL), lambda i, j: (0, 0)),
            pl.BlockSpec((D_MODEL, PROJ_TN), lambda i, j: (0, j)),
            pl.BlockSpec((D_MODEL, DT_PAD), lambda i, j: (0, 0)),
        ],
        out_specs=[
            pl.BlockSpec((tm, PROJ_TN), lambda i, j: (i, j)),
            pl.BlockSpec((tm, DT_PAD), lambda i, j: (i, 0)),
        ],
        out_shape=[
            jax.ShapeDtypeStruct((n, P_COLS), BF16),
            jax.ShapeDtypeStruct((n, DT_PAD), F32),
        ],
        scratch_shapes=[pltpu.VMEM((tm, D_MODEL), BF16)],
        compiler_params=pltpu.CompilerParams(
            dimension_semantics=("arbitrary", "arbitrary"),
            vmem_limit_bytes=VMEM_LIMIT),
        name="proj",
    )(x2d, norm_w, w_main, w_dt)


def _conv_silu(u_bf16, tail, w, b):
    u = u_bf16.astype(F32)
    row8 = lax.broadcasted_iota(jnp.int32, (8, 1), 0)
    acc = b + w[CONV_W - 1:CONV_W, :] * u
    for s in range(1, CONV_W):
        us = pltpu.roll(u, s, axis=0)
        ts = pltpu.roll(tail, s, axis=0)
        head = jnp.where(row8 < s, ts, us[0:8, :])
        us = jnp.concatenate([head, us[8:, :]], axis=0)
        acc = acc + w[CONV_W - 1 - s:CONV_W - s, :] * us
    return _silu(acc), u[CHUNK - 8:, :]


def _ssd_kernel(z_ref, x_ref, b_ref, c_ref, dtr_ref, h0_ref, conv0_ref,
                cw_ref, cb_ref, dtb_ref, alog_ref, dskip_ref, nw_ref,
                yz_ref, hout_ref, tailout_ref,
                ht_ref, tail_ref, xw_ref, y_ref, ea_ref):
    c = pl.program_id(1)
    n_chunks = pl.num_programs(1)

    @pl.when(c == 0)
    def _():
        for g in range(N_GROUPS):
            sl = slice(g * GROUP_W, (g + 1) * GROUP_W)
            ht_ref[:, sl] = h0_ref[0, sl, :].T
        tail_ref[...] = conv0_ref[0]

    xs, tail_x = _conv_silu(x_ref[...], tail_ref[:, 0:D_INNER],
                            cw_ref[:, 0:D_INNER], cb_ref[:, 0:D_INNER])
    bm, tail_b = _conv_silu(b_ref[...], tail_ref[:, D_INNER:D_INNER + BC_W],
                            cw_ref[:, D_INNER:D_INNER + BC_W], cb_ref[:, D_INNER:D_INNER + BC_W])
    cm, tail_c = _conv_silu(c_ref[...], tail_ref[:, D_INNER + BC_W:],
                            cw_ref[:, D_INNER + BC_W:], cb_ref[:, D_INNER + BC_W:])
    tail_ref[:, 0:D_INNER] = tail_x
    tail_ref[:, D_INNER:D_INNER + BC_W] = tail_b
    tail_ref[:, D_INNER + BC_W:] = tail_c
    bm_b = bm.astype(BF16)
    cm_b = cm.astype(BF16)

    dt = _softplus(dtr_ref[...] + dtb_ref[...])
    adt = dt * (-jnp.exp(alog_ref[...]))
    li = lax.broadcasted_iota(jnp.int32, (CHUNK, CHUNK), 0)
    si = lax.broadcasted_iota(jnp.int32, (CHUNK, CHUNK), 1)
    tri = (li >= si).astype(F32)
    acs = jnp.dot(tri, adt, precision=lax.Precision.HIGHEST,
                  preferred_element_type=F32)
    s2 = lax.broadcasted_iota(jnp.int32, (CHUNK, 2 * CHUNK), 0)
    l2 = lax.broadcasted_iota(jnp.int32, (CHUNK, 2 * CHUNK), 1) % CHUNK
    tri_t2 = (l2 >= s2).astype(F32)
    acs_t2 = lax.dot_general(adt, tri_t2, (((0,), (0,)), ((), ())),
                             precision=lax.Precision.HIGHEST,
                             preferred_element_type=F32)

    lane = lax.broadcasted_iota(jnp.int32, (CHUNK, 2 * CHUNK), 1)
    lo = lane < CHUNK
    lo_row = lo[0:1, :]
    tri2 = (lax.broadcasted_iota(jnp.int32, (CHUNK, 2 * CHUNK), 0) >= (lane % CHUNK))

    yoff = []
    cb2 = []
    for g in range(N_GROUPS):
        sl = slice(g * GROUP_W, (g + 1) * GROUP_W)
        ns = slice(g * D_STATE, (g + 1) * D_STATE)
        yoff.append(jnp.dot(cm_b[:, ns], ht_ref[:, sl].astype(BF16),
                            preferred_element_type=F32))
        b2 = jnp.concatenate([bm_b[:, ns], bm_b[:, ns]], axis=0)
        cb2.append(lax.dot_general(cm_b[:, ns], b2, (((1,), (1,)), ((), ())),
                                   preferred_element_type=F32))

    for j in range(SSM_HEADS // 2):
        g = j // (SSM_HEADS // 2 // N_GROUPS)
        r0, r1 = 2 * j, 2 * j + 1
        sl = slice(j * 128, (j + 1) * 128)
        gsl = slice((j % 4) * 128, (j % 4 + 1) * 128)
        u_col = jnp.where(lo, acs[:, r0:r0 + 1], acs[:, r1:r1 + 1])
        v_row = jnp.where(lo_row, acs_t2[r0:r0 + 1, :], acs_t2[r1:r1 + 1, :])
        dtp = jnp.where(lo, dt[:, r0:r0 + 1], dt[:, r1:r1 + 1])
        decay = jnp.exp(jnp.where(tri2, u_col - v_row, -jnp.inf))
        m = (cb2[g] * decay).astype(BF16)
        xs_p = xs[:, sl]
        xdt = xs_p * dtp
        xdt_b = xdt.astype(BF16)
        zero = jnp.zeros_like(xdt_b)
        rhs = jnp.concatenate([jnp.where(lo, xdt_b, zero), jnp.where(lo, zero, xdt_b)], axis=0)
        yd = jnp.dot(m, rhs, preferred_element_type=F32)
        e_u = jnp.exp(u_col)
        y = yd + yoff[g][:, gsl] * e_u + dskip_ref[:, sl] * xs_p
        to_end = jnp.exp(u_col[CHUNK - 1:CHUNK, :] - u_col)
        xw_ref[:, sl] = (xdt * to_end).astype(BF16)
        ea_ref[:, sl] = e_u[CHUNK - 8:CHUNK, :]
        zf = z_ref[:, sl].astype(F32)
        y_ref[:, sl] = y * _silu(zf)

    for g in range(N_GROUPS):
        sl = slice(g * GROUP_W, (g + 1) * GROUP_W)
        ns = slice(g * D_STATE, (g + 1) * D_STATE)
        yg = y_ref[:, sl]
        ms = jnp.mean(yg * yg, axis=-1, keepdims=True)
        yz_ref[:, sl] = ((yg * lax.rsqrt(ms + NORM_EPS)) * nw_ref[:, sl]).astype(BF16)
        upd = lax.dot_general(bm_b[:, ns], xw_ref[:, sl], (((0,), (0,)), ((), ())),
                              preferred_element_type=F32)
        ht_ref[:, sl] = ht_ref[:, sl] * ea_ref[7:8, sl] + upd

    @pl.when(c == n_chunks - 1)
    def _():
        for g in range(N_GROUPS):
            sl = slice(g * GROUP_W, (g + 1) * GROUP_W)
            hout_ref[0, sl, :] = ht_ref[:, sl].T
        tailout_ref[0] = tail_ref[...]


def _ssd(p, dtraw, h0, conv0, conv_w, conv_b, dt_bias, a_log, d_skip_x, ssm_norm_w, nb, nc):
    n = nb * nc * CHUNK
    row = lambda b, c: b * nc + c
    const = lambda b, c: (0, 0)
    return pl.pallas_call(
        _ssd_kernel,
        grid=(nb, nc),
        in_specs=[
            pl.BlockSpec((CHUNK, D_INNER), lambda b, c: (row(b, c), COL_Z // D_INNER)),
            pl.BlockSpec((CHUNK, D_INNER), lambda b, c: (row(b, c), COL_X // D_INNER)),
            pl.BlockSpec((CHUNK, BC_W), lambda b, c: (row(b, c), COL_B // BC_W)),
            pl.BlockSpec((CHUNK, BC_W), lambda b, c: (row(b, c), COL_C // BC_W)),
            pl.BlockSpec((CHUNK, DT_PAD), lambda b, c: (row(b, c), 0)),
            pl.BlockSpec((1, D_INNER, D_STATE), lambda b, c: (b, 0, 0)),
            pl.BlockSpec((1, 8, CONV_DIM), lambda b, c: (b, 0, 0)),
            pl.BlockSpec((CONV_W, CONV_DIM), const),
            pl.BlockSpec((1, CONV_DIM), const),
            pl.BlockSpec((1, DT_PAD), const),
            pl.BlockSpec((1, DT_PAD), const),
            pl.BlockSpec((1, D_INNER), const),
            pl.BlockSpec((1, D_INNER), const),
        ],
        out_specs=[
            pl.BlockSpec((CHUNK, D_INNER), lambda b, c: (row(b, c), 0)),
            pl.BlockSpec((1, D_INNER, D_STATE), lambda b, c: (b, 0, 0)),
            pl.BlockSpec((1, 8, CONV_DIM), lambda b, c: (b, 0, 0)),
        ],
        out_shape=[
            jax.ShapeDtypeStruct((n, D_INNER), BF16),
            jax.ShapeDtypeStruct((nb, D_INNER, D_STATE), F32),
            jax.ShapeDtypeStruct((nb, 8, CONV_DIM), F32),
        ],
        scratch_shapes=[
            pltpu.VMEM((D_STATE, D_INNER), F32),
            pltpu.VMEM((8, CONV_DIM), F32),
            pltpu.VMEM((CHUNK, D_INNER), BF16),
            pltpu.VMEM((CHUNK, D_INNER), F32),
            pltpu.VMEM((8, D_INNER), F32),
        ],
        compiler_params=pltpu.CompilerParams(
            dimension_semantics=("arbitrary", "arbitrary"),
            vmem_limit_bytes=VMEM_LIMIT),
        name="ssd",
    )(p, p, p, p, dtraw, h0, conv0, conv_w, conv_b, dt_bias, a_log, d_skip_x, ssm_norm_w)


ATT_TQ = 256
ATT_TILES = 3
ATT_KEYS = ATT_TILES * ATT_TQ
BIAS_ROW = 1024
KEEP_TILES = LEFT_CHUNKS * CHUNK // ATT_TQ


def _bias_kernel(r_ref, o_ref):
    x = jnp.broadcast_to(r_ref[0], (ATT_TQ, BIAS_ROW))
    tab = pltpu.roll(x, BIAS_ROW - ATT_TQ, axis=1, stride=1, stride_axis=0)[:, :ATT_KEYS]
    qc = lax.broadcasted_iota(jnp.int32, (ATT_TQ, ATT_KEYS), 0) // CHUNK
    kc = lax.broadcasted_iota(jnp.int32, (ATT_TQ, ATT_KEYS), 1) // CHUNK
    ok = (kc >= qc) & (kc <= qc + LEFT_CHUNKS)
    o_ref[0] = jnp.where(ok, tab * LOG2E, NEG_INF)


def _bias_table(rel_row):
    return pl.pallas_call(
        _bias_kernel,
        grid=(ATT_HEADS,),
        in_specs=[pl.BlockSpec((1, 1, BIAS_ROW), lambda h: (h, 0, 0))],
        out_specs=pl.BlockSpec((1, ATT_TQ, ATT_KEYS), lambda h: (h, 0, 0)),
        out_shape=jax.ShapeDtypeStruct((ATT_HEADS, ATT_TQ, ATT_KEYS), F32),
        name="bias_table",
    )(rel_row)


def _softmax_pv(s_tiles, v_tiles):
    def lane_reduce(tiles, combine, reduce):
        if len({t.shape for t in tiles}) == 1:
            tiles = [functools.reduce(combine, tiles)]
        return functools.reduce(combine, [reduce(t, axis=-1, keepdims=True) for t in tiles])

    mx = lane_reduce(s_tiles, jnp.maximum, jnp.max)
    p_tiles = [jnp.exp2(s - mx) for s in s_tiles]
    den = lane_reduce(p_tiles, jnp.add, jnp.sum)
    o = functools.reduce(jnp.add, [jnp.dot(p.astype(BF16), v, preferred_element_type=F32)
                                   for p, v in zip(p_tiles, v_tiles)])
    return o / den


def _attn_prompt_kernel(q_ref, g_ref, k0_ref, k1_ref, k2_ref, v0_ref, v1_ref, v2_ref,
                        bias_ref, o_ref, kp_ref, vp_ref):
    t = pl.program_id(1)
    k_refs = (k0_ref, k1_ref, k2_ref)
    v_refs = (v0_ref, v1_ref, v2_ref)

    @pl.when(t >= pl.num_programs(1) - KEEP_TILES)
    def _():
        kp_ref[0] = k2_ref[...].astype(F32)
        vp_ref[0] = v2_ref[...].astype(F32)

    def body(mask_start):
        lo = lax.broadcasted_iota(jnp.int32, (ATT_TQ, 2 * ATT_HEAD_DIM), 1) < ATT_HEAD_DIM
        for hp in range(ATT_HEADS // 2):
            sl = slice(hp * 128, (hp + 1) * 128)
            qp = q_ref[:, sl]
            zero = jnp.zeros_like(qp)
            outs = []
            for j in range(2):
                h = 2 * hp + j
                qm = jnp.where(lo, qp, zero) if j == 0 else jnp.where(lo, zero, qp)
                s_tiles = []
                for n in range(ATT_TILES):
                    s = lax.dot_general(qm, k_refs[n][:, sl], (((1,), (1,)), ((), ())),
                                        preferred_element_type=F32)
                    s = s + bias_ref[h, :, n * ATT_TQ:(n + 1) * ATT_TQ]
                    if mask_start and n < ATT_TILES - 1:
                        s = jnp.where(t + n >= ATT_TILES - 1, s, NEG_INF)
                    s_tiles.append(s)
                outs.append(_softmax_pv(s_tiles, [v_refs[n][:, sl] for n in range(ATT_TILES)]))
            o_pair = jnp.where(lo, outs[0], outs[1])
            gate = g_ref[:, sl].astype(F32)
            o_ref[:, sl] = (o_pair * _silu(gate)).astype(BF16)

    pl.when(t < ATT_TILES - 1)(functools.partial(body, True))
    pl.when(t >= ATT_TILES - 1)(functools.partial(body, False))


def _attn_prompt(p, bias, nb, seq):
    nt = seq // ATT_TQ

    def kv_spec(n, col):
        return pl.BlockSpec(
            (ATT_TQ, D_ATT),
            lambda b, t: (b * nt + jnp.maximum(t - (ATT_TILES - 1) + n, 0), col))

    in_specs = [
        pl.BlockSpec((ATT_TQ, D_ATT), lambda b, t: (b * nt + t, COL_Q // D_ATT)),
        pl.BlockSpec((ATT_TQ, D_ATT), lambda b, t: (b * nt + t, COL_G // D_ATT)),
    ]
    in_specs += [kv_spec(n, COL_K // D_ATT) for n in range(ATT_TILES)]
    in_specs += [kv_spec(n, COL_V // D_ATT) for n in range(ATT_TILES)]
    in_specs += [pl.BlockSpec((ATT_HEADS, ATT_TQ, ATT_KEYS), lambda b, t: (0, 0, 0),
                              pipeline_mode=pl.Buffered(1))]
    keep_spec = pl.BlockSpec((1, ATT_TQ, D_ATT),
                             lambda b, t: (b, jnp.maximum(t - (nt - KEEP_TILES), 0), 0))
    keep_shape = jax.ShapeDtypeStruct((nb, KEEP_TILES * ATT_TQ, D_ATT), F32)
    return pl.pallas_call(
        _attn_prompt_kernel,
        grid=(nb, nt),
        in_specs=in_specs,
        out_specs=[pl.BlockSpec((ATT_TQ, D_ATT), lambda b, t: (b * nt + t, 0)),
                   keep_spec, keep_spec],
        out_shape=[jax.ShapeDtypeStruct((nb * seq, D_ATT), BF16), keep_shape, keep_shape],
        compiler_params=pltpu.CompilerParams(
            dimension_semantics=("arbitrary", "arbitrary"),
            vmem_limit_bytes=VMEM_LIMIT),
        name="attn_prompt",
    )(p, p, p, p, p, p, p, p, bias)


def _attn_step_kernel(q_ref, g_ref, kn_ref, vn_ref, kc_ref, vc_ref, bias_ref,
                      o_ref, ks_ref, vs_ref):
    keep = LEFT_CHUNKS * CHUNK
    ks_ref[0, 0:keep - CHUNK, :] = kc_ref[0, CHUNK:keep, :]
    ks_ref[0, keep - CHUNK:keep, :] = kn_ref[...].astype(F32)
    vs_ref[0, 0:keep - CHUNK, :] = vc_ref[0, CHUNK:keep, :]
    vs_ref[0, keep - CHUNK:keep, :] = vn_ref[...].astype(F32)
    lo = lax.broadcasted_iota(jnp.int32, (CHUNK, 2 * ATT_HEAD_DIM), 1) < ATT_HEAD_DIM

    for hp in range(ATT_HEADS // 2):
        sl = slice(hp * 128, (hp + 1) * 128)
        qp = q_ref[:, sl]
        k_tiles = [kc_ref[0, :, sl].astype(BF16), kn_ref[:, sl]]
        v_tiles = [vc_ref[0, :, sl].astype(BF16), vn_ref[:, sl]]
        zero = jnp.zeros_like(qp)
        outs = []
        for j in range(2):
            h = 2 * hp + j
            qm = jnp.where(lo, qp, zero) if j == 0 else jnp.where(lo, zero, qp)
            s_tiles = []
            for n, (k0, k1) in enumerate(((0, keep), (keep, BAND))):
                s = lax.dot_general(qm, k_tiles[n], (((1,), (1,)), ((), ())),
                                    preferred_element_type=F32)
                s_tiles.append(s + bias_ref[h, 0:CHUNK, k0:k1])
            outs.append(_softmax_pv(s_tiles, v_tiles))
        o_pair = jnp.where(lo, outs[0], outs[1])
        gate = g_ref[:, sl].astype(F32)
        o_ref[:, sl] = (o_pair * _silu(gate)).astype(BF16)


def _attn_step(p, cache_k, cache_v, bias, nb):
    keep = LEFT_CHUNKS * CHUNK
    new = lambda col: pl.BlockSpec((CHUNK, D_ATT), lambda b: (b, col))
    cache = pl.BlockSpec((1, keep, D_ATT), lambda b: (b, 0, 0))
    return pl.pallas_call(
        _attn_step_kernel,
        grid=(nb,),
        in_specs=[new(COL_Q // D_ATT), new(COL_G // D_ATT), new(COL_K // D_ATT),
                  new(COL_V // D_ATT), cache, cache,
                  pl.BlockSpec((ATT_HEADS, ATT_TQ, ATT_KEYS), lambda b: (0, 0, 0),
                               pipeline_mode=pl.Buffered(1))],
        out_specs=[pl.BlockSpec((CHUNK, D_ATT), lambda b: (b, 0)), cache, cache],
        out_shape=[jax.ShapeDtypeStruct((nb * CHUNK, D_ATT), BF16),
                   jax.ShapeDtypeStruct((nb, keep, D_ATT), F32),
                   jax.ShapeDtypeStruct((nb, keep, D_ATT), F32)],
        compiler_params=pltpu.CompilerParams(
            dimension_semantics=("arbitrary",),
            vmem_limit_bytes=VMEM_LIMIT),
        name="attn_step",
    )(p, p, p, p, cache_k, cache_v, bias)


OUT_TM = 512


def _out_kernel(yz_ref, og_ref, gs_ref, ga_ref, x_ref, wos_ref, woa_ref, wo_ref, fnw_ref, y_ref):
    y_ssm = jnp.dot(yz_ref[...], wos_ref[...], preferred_element_type=F32)
    y_att = jnp.dot(og_ref[...], woa_ref[...], preferred_element_type=F32)
    merged = (_sigmoid(gs_ref[...].astype(F32)) * y_ssm
              + _sigmoid(ga_ref[...].astype(F32)) * y_att)
    h = x_ref[...] + jnp.dot(merged.astype(BF16), wo_ref[...], preferred_element_type=F32)
    ms = jnp.mean(h * h, axis=-1, keepdims=True)
    y_ref[...] = (h * lax.rsqrt(ms + NORM_EPS)) * fnw_ref[...]


def _out(yz, og, p, x2d, w_out_ssm, w_out_att, w_o, final_norm_w):
    n = x2d.shape[0]
    tm = min(OUT_TM, n)
    const = lambda i: (0, 0)
    return pl.pallas_call(
        _out_kernel,
        grid=(n // tm,),
        in_specs=[
            pl.BlockSpec((tm, D_INNER), lambda i: (i, 0)),
            pl.BlockSpec((tm, D_ATT), lambda i: (i, 0)),
            pl.BlockSpec((tm, D_MODEL), lambda i: (i, COL_GS // D_MODEL)),
            pl.BlockSpec((tm, D_MODEL), lambda i: (i, COL_GA // D_MODEL)),
            pl.BlockSpec((tm, D_MODEL), lambda i: (i, 0)),
            pl.BlockSpec((D_INNER, D_MODEL), const),
            pl.BlockSpec((D_ATT, D_MODEL), const),
            pl.BlockSpec((D_MODEL, D_MODEL), const),
            pl.BlockSpec((1, D_MODEL), const),
        ],
        out_specs=pl.BlockSpec((tm, D_MODEL), lambda i: (i, 0)),
        out_shape=jax.ShapeDtypeStruct((n, D_MODEL), F32),
        compiler_params=pltpu.CompilerParams(
            dimension_semantics=("arbitrary",),
            vmem_limit_bytes=VMEM_LIMIT),
        name="out",
    )(yz, og, p, p, x2d, w_out_ssm, w_out_att, w_o, final_norm_w)


def _rel_bias_rows(rel_bias):
    d = ATT_TILES * ATT_TQ - jnp.arange(BIAS_ROW)
    idx = jnp.clip(d, -REL_CLIP, REL_CLIP) + REL_CLIP
    return rel_bias[:, idx].astype(F32).reshape(ATT_HEADS, 1, BIAS_ROW)


def kernel(x_prompt, x_sample, state_ssm, state_conv, cache_k, cache_v, norm_w, w_in, conv_w,
           conv_b, dt_bias, a_log, d_skip, ssm_norm_w, w_out_ssm, rel_bias, w_out_att, w_o,
           final_norm_w):
    bp, lp, _ = x_prompt.shape
    bs, ls, _ = x_sample.shape
    ncp = lp // CHUNK
    assert ls == CHUNK and cache_k.shape[2] == LEFT_CHUNKS * CHUNK

    w = w_in[0]
    o = 0
    parts = {}
    for name, size in (("z", D_INNER), ("xbc", CONV_DIM), ("dt", SSM_HEADS), ("q", D_ATT),
                       ("k", D_ATT), ("v", D_ATT), ("g", D_ATT), ("gs", D_MODEL), ("ga", D_MODEL)):
        parts[name] = w[:, o:o + size]
        o += size
    w_main = jnp.concatenate(
        [parts["z"], parts["xbc"], parts["q"] * (ATT_HEAD_DIM ** -0.5 * LOG2E), parts["k"], parts["v"],
         parts["g"], parts["gs"], parts["ga"]], axis=1).astype(BF16)
    w_dt = jnp.pad(parts["dt"], ((0, 0), (0, DT_PAD - SSM_HEADS))).astype(BF16)
    nw = norm_w[0].reshape(1, D_MODEL)
    cw = conv_w[0]
    cb = conv_b[0].reshape(1, CONV_DIM)
    dtb = jnp.pad(dt_bias[0], (0, DT_PAD - SSM_HEADS)).reshape(1, DT_PAD)
    alog = jnp.pad(a_log[0], (0, DT_PAD - SSM_HEADS)).reshape(1, DT_PAD)
    dskip_x = jnp.repeat(d_skip[0], SSM_HEAD_DIM).reshape(1, D_INNER)
    snw = ssm_norm_w[0].reshape(1, D_INNER)
    wos = w_out_ssm[0].astype(BF16)
    woa = w_out_att[0].astype(BF16)
    wo = w_o[0].astype(BF16)
    fnw = final_norm_w.reshape(1, D_MODEL)
    bias = _bias_table(_rel_bias_rows(rel_bias[0]))

    ssd_params = (cw, cb, dtb, alog, dskip_x, snw)
    out_params = (wos, woa, wo, fnw)

    xp = x_prompt.reshape(bp * lp, D_MODEL)
    p_p, dt_p = _proj(xp, nw, w_main, w_dt)
    h0_p = jnp.zeros((bp, D_INNER, D_STATE), F32)
    conv0_p = jnp.zeros((bp, 8, CONV_DIM), F32)
    yz_p, ssm_p, tail_p = _ssd(p_p, dt_p, h0_p, conv0_p, *ssd_params, nb=bp, nc=ncp)
    og_p, k_p, v_p = _attn_prompt(p_p, bias, nb=bp, seq=lp)
    y_p = _out(yz_p, og_p, p_p, xp, *out_params).reshape(bp, lp, D_MODEL)
    keep = LEFT_CHUNKS * CHUNK
    k_p = k_p.reshape(1, bp, keep, ATT_HEADS, ATT_HEAD_DIM)
    v_p = v_p.reshape(1, bp, keep, ATT_HEADS, ATT_HEAD_DIM)

    xs = x_sample.reshape(bs * ls, D_MODEL)
    p_s, dt_s = _proj(xs, nw, w_main, w_dt)
    h0_s = state_ssm[0].reshape(bs, D_INNER, D_STATE)
    conv0_s = jnp.pad(state_conv[0], ((0, 0), (8 - (CONV_W - 1), 0), (0, 0)))
    yz_s, ssm_s, tail_s = _ssd(p_s, dt_s, h0_s, conv0_s, *ssd_params, nb=bs, nc=1)
    og_s, k_s, v_s = _attn_step(p_s, cache_k[0].reshape(bs, keep, D_ATT),
                                cache_v[0].reshape(bs, keep, D_ATT), bias, nb=bs)
    y_s = _out(yz_s, og_s, p_s, xs, *out_params).reshape(bs, ls, D_MODEL)
    k_s = k_s.reshape(1, bs, keep, ATT_HEADS, ATT_HEAD_DIM)
    v_s = v_s.reshape(1, bs, keep, ATT_HEADS, ATT_HEAD_DIM)

    def states(ssm, tail, nb):
        return (ssm.reshape(1, nb, SSM_HEADS, SSM_HEAD_DIM, D_STATE),
                tail[:, 8 - (CONV_W - 1):, :].reshape(1, nb, CONV_W - 1, CONV_DIM))

    ssm_p5, conv_p4 = states(ssm_p, tail_p, bp)
    ssm_s5, conv_s4 = states(ssm_s, tail_s, bs)
    return (y_p, y_s, ssm_p5, conv_p4, k_p, v_p, ssm_s5, conv_s4, k_s, v_s)
```

```python
import functools

import jax
import jax.numpy as jnp
from jax import lax
from jax.experimental import pallas as pl
from jax.experimental.pallas import tpu as pltpu

D_MODEL = 1024
CHUNK = 64
D_INNER = 2048
SSM_HEAD_DIM = 64
SSM_HEADS = 32
N_GROUPS = 4
D_STATE = 128
GROUP_W = D_INNER // N_GROUPS
BC_W = N_GROUPS * D_STATE
CONV_W = 4
CONV_DIM = D_INNER + 2 * BC_W
ATT_HEADS = 16
ATT_HEAD_DIM = 64
D_ATT = 1024
LEFT_CHUNKS = 8
BAND_CHUNKS = LEFT_CHUNKS + 1
BAND = BAND_CHUNKS * CHUNK
REL_CLIP = 256
NORM_EPS = 1e-5
NEG_INF = -1e30
LOG2E = 1.4426950408889634

COL_Z = 0
COL_X = 2048
COL_B = 4096
COL_C = 4608
COL_Q = 5120
COL_V = 6144
COL_G = 7168
COL_GS = 8192
COL_GA = 9216
P_COLS = 10240
DT_PAD = 128

VMEM_LIMIT = 56 * 1024 * 1024

F32 = jnp.float32
BF16 = jnp.bfloat16


def _sigmoid(x):
    return 0.5 * jnp.tanh(0.5 * x) + 0.5


def _silu(x):
    h = 0.5 * x
    return h * jnp.tanh(h) + h


def _softplus(x):
    return jnp.maximum(x, 0.0) + jnp.log(1.0 + jnp.exp(-jnp.abs(x)))


PROJ_TM = 1024
PROJ_TN = 2560


def _proj_kernel(x_ref, nw_ref, w_ref, wdt_ref, wkt_ref, p_ref, dt_ref, kt_ref, xn_ref):
    @pl.when(pl.program_id(1) == 0)
    def _():
        x = x_ref[...]
        ms = jnp.mean(x * x, axis=-1, keepdims=True)
        xn = (x * lax.rsqrt(ms + NORM_EPS)) * nw_ref[...]
        xn_ref[...] = xn.astype(BF16)
        dt_ref[...] = jnp.dot(xn_ref[...], wdt_ref[...], preferred_element_type=F32)
        kt_ref[...] = lax.dot_general(wkt_ref[...], xn_ref[...], (((1,), (1,)), ((), ())),
                                      preferred_element_type=F32).astype(BF16)

    p_ref[...] = jnp.dot(xn_ref[...], w_ref[...], preferred_element_type=F32).astype(BF16)


def _proj(x2d, norm_w, w_main, w_dt, w_kt):
    n = x2d.shape[0]
    tm = min(PROJ_TM, n)
    return pl.pallas_call(
        _proj_kernel,
        grid=(n // tm, P_COLS // PROJ_TN),
        in_specs=[
            pl.BlockSpec((tm, D_MODEL), lambda i, j: (i, 0)),
            pl.BlockSpec((1, D_MODEL), lambda i, j: (0, 0)),
            pl.BlockSpec((D_MODEL, PROJ_TN), lambda i, j: (0, j)),
            pl.BlockSpec((D_MODEL, DT_PAD), lambda i, j: (0, 0)),
            pl.BlockSpec((D_ATT, D_MODEL), lambda i, j: (0, 0)),
        ],
        out_specs=[
            pl.BlockSpec((tm, PROJ_TN), lambda i, j: (i, j)),
            pl.BlockSpec((tm, DT_PAD), lambda i, j: (i, 0)),
            pl.BlockSpec((D_ATT, tm), lambda i, j: (0, i)),
        ],
        out_shape=[
            jax.ShapeDtypeStruct((n, P_COLS), BF16),
            jax.ShapeDtypeStruct((n, DT_PAD), F32),
            jax.ShapeDtypeStruct((D_ATT, n), BF16),
        ],
        scratch_shapes=[pltpu.VMEM((tm, D_MODEL), BF16)],
        compiler_params=pltpu.CompilerParams(
            dimension_semantics=("arbitrary", "arbitrary"),
            vmem_limit_bytes=VMEM_LIMIT),
        name="proj",
    )(x2d, norm_w, w_main, w_dt, w_kt)


def _conv_silu(u_bf16, tail, w, b):
    u = u_bf16.astype(F32)
    row8 = lax.broadcasted_iota(jnp.int32, (8, 1), 0)
    acc = b + w[CONV_W - 1:CONV_W, :] * u
    for s in range(1, CONV_W):
        us = pltpu.roll(u, s, axis=0)
        ts = pltpu.roll(tail, s, axis=0)
        head = jnp.where(row8 < s, ts, us[0:8, :])
        us = jnp.concatenate([head, us[8:, :]], axis=0)
        acc = acc + w[CONV_W - 1 - s:CONV_W - s, :] * us
    return _silu(acc), u[CHUNK - 8:, :]


def _ssd_kernel(z_ref, x_ref, b_ref, c_ref, dtr_ref, h0_ref, conv0_ref,
                cw_ref, cb_ref, dtb_ref, alog_ref, dskip_ref, nw_ref,
                yz_ref, hout_ref, tailout_ref,
                ht_ref, tail_ref, xw_ref, y_ref, ea_ref):
    c = pl.program_id(1)
    n_chunks = pl.num_programs(1)

    @pl.when(c == 0)
    def _():
        for g in range(N_GROUPS):
            sl = slice(g * GROUP_W, (g + 1) * GROUP_W)
            ht_ref[:, sl] = h0_ref[0, sl, :].T
        tail_ref[...] = conv0_ref[0]

    xs, tail_x = _conv_silu(x_ref[...], tail_ref[:, 0:D_INNER],
                            cw_ref[:, 0:D_INNER], cb_ref[:, 0:D_INNER])
    bm, tail_b = _conv_silu(b_ref[...], tail_ref[:, D_INNER:D_INNER + BC_W],
                            cw_ref[:, D_INNER:D_INNER + BC_W], cb_ref[:, D_INNER:D_INNER + BC_W])
    cm, tail_c = _conv_silu(c_ref[...], tail_ref[:, D_INNER + BC_W:],
                            cw_ref[:, D_INNER + BC_W:], cb_ref[:, D_INNER + BC_W:])
    tail_ref[:, 0:D_INNER] = tail_x
    tail_ref[:, D_INNER:D_INNER + BC_W] = tail_b
    tail_ref[:, D_INNER + BC_W:] = tail_c
    bm_b = bm.astype(BF16)
    cm_b = cm.astype(BF16)

    dt = _softplus(dtr_ref[...] + dtb_ref[...])
    adt = dt * (-jnp.exp(alog_ref[...]))
    li = lax.broadcasted_iota(jnp.int32, (CHUNK, CHUNK), 0)
    si = lax.broadcasted_iota(jnp.int32, (CHUNK, CHUNK), 1)
    tri = (li >= si).astype(F32)
    acs = jnp.dot(tri, adt, precision=lax.Precision.HIGHEST,
                  preferred_element_type=F32)
    s2 = lax.broadcasted_iota(jnp.int32, (CHUNK, 2 * CHUNK), 0)
    l2 = lax.broadcasted_iota(jnp.int32, (CHUNK, 2 * CHUNK), 1) % CHUNK
    tri_t2 = (l2 >= s2).astype(F32)
    acs_t2 = lax.dot_general(adt, tri_t2, (((0,), (0,)), ((), ())),
                             precision=lax.Precision.HIGHEST,
                             preferred_element_type=F32)

    lane = lax.broadcasted_iota(jnp.int32, (CHUNK, 2 * CHUNK), 1)
    lo = lane < CHUNK
    lo_row = lo[0:1, :]
    tri2 = (lax.broadcasted_iota(jnp.int32, (CHUNK, 2 * CHUNK), 0) >= (lane % CHUNK))

    yoff = []
    cb2 = []
    for g in range(N_GROUPS):
        sl = slice(g * GROUP_W, (g + 1) * GROUP_W)
        ns = slice(g * D_STATE, (g + 1) * D_STATE)
        yoff.append(jnp.dot(cm_b[:, ns], ht_ref[:, sl].astype(BF16),
                            preferred_element_type=F32))
        b2 = jnp.concatenate([bm_b[:, ns], bm_b[:, ns]], axis=0)
        cb2.append(lax.dot_general(cm_b[:, ns], b2, (((1,), (1,)), ((), ())),
                                   preferred_element_type=F32))

    for j in range(SSM_HEADS // 2):
        g = j // (SSM_HEADS // 2 // N_GROUPS)
        r0, r1 = 2 * j, 2 * j + 1
        sl = slice(j * 128, (j + 1) * 128)
        gsl = slice((j % 4) * 128, (j % 4 + 1) * 128)
        u_col = jnp.where(lo, acs[:, r0:r0 + 1], acs[:, r1:r1 + 1])
        v_row = jnp.where(lo_row, acs_t2[r0:r0 + 1, :], acs_t2[r1:r1 + 1, :])
        dtp = jnp.where(lo, dt[:, r0:r0 + 1], dt[:, r1:r1 + 1])
        decay = jnp.exp(jnp.where(tri2, u_col - v_row, -jnp.inf))
        m = (cb2[g] * decay).astype(BF16)
        xs_p = xs[:, sl]
        xdt = xs_p * dtp
        xdt_b = xdt.astype(BF16)
        zero = jnp.zeros_like(xdt_b)
        rhs = jnp.concatenate([jnp.where(lo, xdt_b, zero), jnp.where(lo, zero, xdt_b)], axis=0)
        yd = jnp.dot(m, rhs, preferred_element_type=F32)
        e_u = jnp.exp(u_col)
        y = yd + yoff[g][:, gsl] * e_u + dskip_ref[:, sl] * xs_p
        to_end = jnp.exp(u_col[CHUNK - 1:CHUNK, :] - u_col)
        xw_ref[:, sl] = (xdt * to_end).astype(BF16)
        ea_ref[:, sl] = e_u[CHUNK - 8:CHUNK, :]
        zf = z_ref[:, sl].astype(F32)
        y_ref[:, sl] = y * _silu(zf)

    for g in range(N_GROUPS):
        sl = slice(g * GROUP_W, (g + 1) * GROUP_W)
        ns = slice(g * D_STATE, (g + 1) * D_STATE)
        yg = y_ref[:, sl]
        ms = jnp.mean(yg * yg, axis=-1, keepdims=True)
        yz_ref[:, sl] = ((yg * lax.rsqrt(ms + NORM_EPS)) * nw_ref[:, sl]).astype(BF16)
        upd = lax.dot_general(bm_b[:, ns], xw_ref[:, sl], (((0,), (0,)), ((), ())),
                              preferred_element_type=F32)
        ht_ref[:, sl] = ht_ref[:, sl] * ea_ref[7:8, sl] + upd

    @pl.when(c == n_chunks - 1)
    def _():
        for g in range(N_GROUPS):
            sl = slice(g * GROUP_W, (g + 1) * GROUP_W)
            hout_ref[0, sl, :] = ht_ref[:, sl].T
        tailout_ref[0] = tail_ref[...]


def _ssd(p, dtraw, h0, conv0, conv_w, conv_b, dt_bias, a_log, d_skip_x, ssm_norm_w, nb, nc):
    n = nb * nc * CHUNK
    row = lambda b, c: b * nc + c
    const = lambda b, c: (0, 0)
    return pl.pallas_call(
        _ssd_kernel,
        grid=(nb, nc),
        in_specs=[
            pl.BlockSpec((CHUNK, D_INNER), lambda b, c: (row(b, c), COL_Z // D_INNER)),
            pl.BlockSpec((CHUNK, D_INNER), lambda b, c: (row(b, c), COL_X // D_INNER)),
            pl.BlockSpec((CHUNK, BC_W), lambda b, c: (row(b, c), COL_B // BC_W)),
            pl.BlockSpec((CHUNK, BC_W), lambda b, c: (row(b, c), COL_C // BC_W)),
            pl.BlockSpec((CHUNK, DT_PAD), lambda b, c: (row(b, c), 0)),
            pl.BlockSpec((1, D_INNER, D_STATE), lambda b, c: (b, 0, 0)),
            pl.BlockSpec((1, 8, CONV_DIM), lambda b, c: (b, 0, 0)),
            pl.BlockSpec((CONV_W, CONV_DIM), const),
            pl.BlockSpec((1, CONV_DIM), const),
            pl.BlockSpec((1, DT_PAD), const),
            pl.BlockSpec((1, DT_PAD), const),
            pl.BlockSpec((1, D_INNER), const),
            pl.BlockSpec((1, D_INNER), const),
        ],
        out_specs=[
            pl.BlockSpec((CHUNK, D_INNER), lambda b, c: (row(b, c), 0)),
            pl.BlockSpec((1, D_INNER, D_STATE), lambda b, c: (b, 0, 0)),
            pl.BlockSpec((1, 8, CONV_DIM), lambda b, c: (b, 0, 0)),
        ],
        out_shape=[
            jax.ShapeDtypeStruct((n, D_INNER), BF16),
            jax.ShapeDtypeStruct((nb, D_INNER, D_STATE), F32),
            jax.ShapeDtypeStruct((nb, 8, CONV_DIM), F32),
        ],
        scratch_shapes=[
            pltpu.VMEM((D_STATE, D_INNER), F32),
            pltpu.VMEM((8, CONV_DIM), F32),
            pltpu.VMEM((CHUNK, D_INNER), BF16),
            pltpu.VMEM((CHUNK, D_INNER), F32),
            pltpu.VMEM((8, D_INNER), F32),
        ],
        compiler_params=pltpu.CompilerParams(
            dimension_semantics=("arbitrary", "arbitrary"),
            vmem_limit_bytes=VMEM_LIMIT),
        name="ssd",
    )(p, p, p, p, dtraw, h0, conv0, conv_w, conv_b, dt_bias, a_log, d_skip_x, ssm_norm_w)


ATT_TQ = 256
ATT_TILES = 3
ATT_KEYS = ATT_TILES * ATT_TQ
BIAS_ROW = 1024
KEEP_TILES = LEFT_CHUNKS * CHUNK // ATT_TQ


def _bias_kernel(r_ref, o_ref):
    x = jnp.broadcast_to(r_ref[0], (ATT_TQ, BIAS_ROW))
    tab = pltpu.roll(x, BIAS_ROW - ATT_TQ, axis=1, stride=1, stride_axis=0)[:, :ATT_KEYS]
    qc = lax.broadcasted_iota(jnp.int32, (ATT_TQ, ATT_KEYS), 0) // CHUNK
    kc = lax.broadcasted_iota(jnp.int32, (ATT_TQ, ATT_KEYS), 1) // CHUNK
    ok = (kc >= qc) & (kc <= qc + LEFT_CHUNKS)
    o_ref[0] = jnp.where(ok, tab * LOG2E, NEG_INF)


def _bias_table(rel_row):
    return pl.pallas_call(
        _bias_kernel,
        grid=(ATT_HEADS,),
        in_specs=[pl.BlockSpec((1, 1, BIAS_ROW), lambda h: (h, 0, 0))],
        out_specs=pl.BlockSpec((1, ATT_TQ, ATT_KEYS), lambda h: (h, 0, 0)),
        out_shape=jax.ShapeDtypeStruct((ATT_HEADS, ATT_TQ, ATT_KEYS), F32),
        name="bias_table",
    )(rel_row)


def _softmax_pv(s_tiles, v_tiles):
    def lane_reduce(tiles, combine, reduce):
        if len({t.shape for t in tiles}) == 1:
            tiles = [functools.reduce(combine, tiles)]
        return functools.reduce(combine, [reduce(t, axis=-1, keepdims=True) for t in tiles])

    mx = lane_reduce(s_tiles, jnp.maximum, jnp.max)
    p_tiles = [jnp.exp2(s - mx) for s in s_tiles]
    den = lane_reduce(p_tiles, jnp.add, jnp.sum)
    o = functools.reduce(jnp.add, [jnp.dot(p.astype(BF16), v, preferred_element_type=F32)
                                   for p, v in zip(p_tiles, v_tiles)])
    return o / den


def _attn_prompt_kernel(q_ref, g_ref, kt0_ref, kt1_ref, kt2_ref, v0_ref, v1_ref, v2_ref,
                        bias_ref, o_ref, kp_ref, vp_ref):
    t = pl.program_id(1)
    kt_refs = (kt0_ref, kt1_ref, kt2_ref)
    v_refs = (v0_ref, v1_ref, v2_ref)

    @pl.when(t >= pl.num_programs(1) - KEEP_TILES)
    def _():
        kp_ref[0] = kt2_ref[...].astype(F32).T
        vp_ref[0] = v2_ref[...].astype(F32)

    def body(mask_start):
        lo = lax.broadcasted_iota(jnp.int32, (ATT_TQ, 2 * ATT_HEAD_DIM), 1) < ATT_HEAD_DIM
        for hp in range(ATT_HEADS // 2):
            sl = slice(hp * 128, (hp + 1) * 128)
            qp = q_ref[:, sl]
            zero = jnp.zeros_like(qp)
            q2 = jnp.concatenate([jnp.where(lo, qp, zero), jnp.where(lo, zero, qp)], axis=0)
            s_tiles = []
            for n in range(ATT_TILES):
                ks = slice(n * ATT_TQ, (n + 1) * ATT_TQ)
                s = jnp.dot(q2, kt_refs[n][sl, :], preferred_element_type=F32)
                s = s + jnp.concatenate([bias_ref[2 * hp, :, ks], bias_ref[2 * hp + 1, :, ks]],
                                        axis=0)
                if mask_start and n < ATT_TILES - 1:
                    s = jnp.where(t + n >= ATT_TILES - 1, s, NEG_INF)
                s_tiles.append(s)
            o2 = _softmax_pv(s_tiles, [v_refs[n][:, sl] for n in range(ATT_TILES)])
            o_pair = jnp.where(lo, o2[:ATT_TQ], o2[ATT_TQ:])
            gate = g_ref[:, sl].astype(F32)
            o_ref[:, sl] = (o_pair * _silu(gate)).astype(BF16)

    pl.when(t < ATT_TILES - 1)(functools.partial(body, True))
    pl.when(t >= ATT_TILES - 1)(functools.partial(body, False))


def _attn_prompt(p, kt, bias, nb, seq):
    nt = seq // ATT_TQ

    def tile_row(n):
        return lambda b, t: b * nt + jnp.maximum(t - (ATT_TILES - 1) + n, 0)

    in_specs = [
        pl.BlockSpec((ATT_TQ, D_ATT), lambda b, t: (b * nt + t, COL_Q // D_ATT)),
        pl.BlockSpec((ATT_TQ, D_ATT), lambda b, t: (b * nt + t, COL_G // D_ATT)),
    ]
    in_specs += [pl.BlockSpec((D_ATT, ATT_TQ), lambda b, t, r=tile_row(n): (0, r(b, t)))
                 for n in range(ATT_TILES)]
    in_specs += [pl.BlockSpec((ATT_TQ, D_ATT), lambda b, t, r=tile_row(n): (r(b, t), COL_V // D_ATT))
                 for n in range(ATT_TILES)]
    in_specs += [pl.BlockSpec((ATT_HEADS, ATT_TQ, ATT_KEYS), lambda b, t: (0, 0, 0),
                              pipeline_mode=pl.Buffered(1))]
    keep_spec = pl.BlockSpec((1, ATT_TQ, D_ATT),
                             lambda b, t: (b, jnp.maximum(t - (nt - KEEP_TILES), 0), 0))
    keep_shape = jax.ShapeDtypeStruct((nb, KEEP_TILES * ATT_TQ, D_ATT), F32)
    return pl.pallas_call(
        _attn_prompt_kernel,
        grid=(nb, nt),
        in_specs=in_specs,
        out_specs=[pl.BlockSpec((ATT_TQ, D_ATT), lambda b, t: (b * nt + t, 0)),
                   keep_spec, keep_spec],
        out_shape=[jax.ShapeDtypeStruct((nb * seq, D_ATT), BF16), keep_shape, keep_shape],
        compiler_params=pltpu.CompilerParams(
            dimension_semantics=("arbitrary", "arbitrary"),
            vmem_limit_bytes=VMEM_LIMIT),
        name="attn_prompt",
    )(p, p, kt, kt, kt, p, p, p, bias)


def _attn_step_kernel(q_ref, g_ref, kn_ref, vn_ref, kc_ref, vc_ref, bias_ref,
                      o_ref, ks_ref, vs_ref):
    keep = LEFT_CHUNKS * CHUNK
    ks_ref[0, 0:keep - CHUNK, :] = kc_ref[0, CHUNK:keep, :]
    ks_ref[0, keep - CHUNK:keep, :] = kn_ref[...].astype(F32)
    vs_ref[0, 0:keep - CHUNK, :] = vc_ref[0, CHUNK:keep, :]
    vs_ref[0, keep - CHUNK:keep, :] = vn_ref[...].astype(F32)
    lo = lax.broadcasted_iota(jnp.int32, (CHUNK, 2 * ATT_HEAD_DIM), 1) < ATT_HEAD_DIM

    for hp in range(ATT_HEADS // 2):
        sl = slice(hp * 128, (hp + 1) * 128)
        qp = q_ref[:, sl]
        k_tiles = [kc_ref[0, :, sl].astype(BF16), kn_ref[:, sl]]
        v_tiles = [vc_ref[0, :, sl].astype(BF16), vn_ref[:, sl]]
        zero = jnp.zeros_like(qp)
        outs = []
        for j in range(2):
            h = 2 * hp + j
            qm = jnp.where(lo, qp, zero) if j == 0 else jnp.where(lo, zero, qp)
            s_tiles = []
            for n, (k0, k1) in enumerate(((0, keep), (keep, BAND))):
                s = lax.dot_general(qm, k_tiles[n], (((1,), (1,)), ((), ())),
                                    preferred_element_type=F32)
                s_tiles.append(s + bias_ref[h, 0:CHUNK, k0:k1])
            outs.append(_softmax_pv(s_tiles, v_tiles))
        o_pair = jnp.where(lo, outs[0], outs[1])
        gate = g_ref[:, sl].astype(F32)
        o_ref[:, sl] = (o_pair * _silu(gate)).astype(BF16)


def _attn_step(p, k_new, cache_k, cache_v, bias, nb):
    keep = LEFT_CHUNKS * CHUNK
    new = lambda col: pl.BlockSpec((CHUNK, D_ATT), lambda b: (b, col))
    cache = pl.BlockSpec((1, keep, D_ATT), lambda b: (b, 0, 0))
    return pl.pallas_call(
        _attn_step_kernel,
        grid=(nb,),
        in_specs=[new(COL_Q // D_ATT), new(COL_G // D_ATT), new(0),
                  new(COL_V // D_ATT), cache, cache,
                  pl.BlockSpec((ATT_HEADS, ATT_TQ, ATT_KEYS), lambda b: (0, 0, 0),
                               pipeline_mode=pl.Buffered(1))],
        out_specs=[pl.BlockSpec((CHUNK, D_ATT), lambda b: (b, 0)), cache, cache],
        out_shape=[jax.ShapeDtypeStruct((nb * CHUNK, D_ATT), BF16),
                   jax.ShapeDtypeStruct((nb, keep, D_ATT), F32),
                   jax.ShapeDtypeStruct((nb, keep, D_ATT), F32)],
        compiler_params=pltpu.CompilerParams(
            dimension_semantics=("arbitrary",),
            vmem_limit_bytes=VMEM_LIMIT),
        name="attn_step",
    )(p, p, k_new, p, cache_k, cache_v, bias)


OUT_TM = 512


def _out_kernel(yz_ref, og_ref, gs_ref, ga_ref, x_ref, wos_ref, woa_ref, wo_ref, fnw_ref, y_ref):
    y_ssm = jnp.dot(yz_ref[...], wos_ref[...], preferred_element_type=F32)
    y_att = jnp.dot(og_ref[...], woa_ref[...], preferred_element_type=F32)
    merged = (_sigmoid(gs_ref[...].astype(F32)) * y_ssm
              + _sigmoid(ga_ref[...].astype(F32)) * y_att)
    h = x_ref[...] + jnp.dot(merged.astype(BF16), wo_ref[...], preferred_element_type=F32)
    ms = jnp.mean(h * h, axis=-1, keepdims=True)
    y_ref[...] = (h * lax.rsqrt(ms + NORM_EPS)) * fnw_ref[...]


def _out(yz, og, p, x2d, w_out_ssm, w_out_att, w_o, final_norm_w):
    n = x2d.shape[0]
    tm = min(OUT_TM, n)
    const = lambda i: (0, 0)
    return pl.pallas_call(
        _out_kernel,
        grid=(n // tm,),
        in_specs=[
            pl.BlockSpec((tm, D_INNER), lambda i: (i, 0)),
            pl.BlockSpec((tm, D_ATT), lambda i: (i, 0)),
            pl.BlockSpec((tm, D_MODEL), lambda i: (i, COL_GS // D_MODEL)),
            pl.BlockSpec((tm, D_MODEL), lambda i: (i, COL_GA // D_MODEL)),
            pl.BlockSpec((tm, D_MODEL), lambda i: (i, 0)),
            pl.BlockSpec((D_INNER, D_MODEL), const),
            pl.BlockSpec((D_ATT, D_MODEL), const),
            pl.BlockSpec((D_MODEL, D_MODEL), const),
            pl.BlockSpec((1, D_MODEL), const),
        ],
        out_specs=pl.BlockSpec((tm, D_MODEL), lambda i: (i, 0)),
        out_shape=jax.ShapeDtypeStruct((n, D_MODEL), F32),
        compiler_params=pltpu.CompilerParams(
            dimension_semantics=("arbitrary",),
            vmem_limit_bytes=VMEM_LIMIT),
        name="out",
    )(yz, og, p, p, x2d, w_out_ssm, w_out_att, w_o, final_norm_w)


def _rel_bias_rows(rel_bias):
    d = ATT_TILES * ATT_TQ - jnp.arange(BIAS_ROW)
    idx = jnp.clip(d, -REL_CLIP, REL_CLIP) + REL_CLIP
    return rel_bias[:, idx].astype(F32).reshape(ATT_HEADS, 1, BIAS_ROW)


def kernel(x_prompt, x_sample, state_ssm, state_conv, cache_k, cache_v, norm_w, w_in, conv_w,
           conv_b, dt_bias, a_log, d_skip, ssm_norm_w, w_out_ssm, rel_bias, w_out_att, w_o,
           final_norm_w):
    bp, lp, _ = x_prompt.shape
    bs, ls, _ = x_sample.shape
    ncp = lp // CHUNK
    assert ls == CHUNK and cache_k.shape[2] == LEFT_CHUNKS * CHUNK

    w = w_in[0]
    o = 0
    parts = {}
    for name, size in (("z", D_INNER), ("xbc", CONV_DIM), ("dt", SSM_HEADS), ("q", D_ATT),
                       ("k", D_ATT), ("v", D_ATT), ("g", D_ATT), ("gs", D_MODEL), ("ga", D_MODEL)):
        parts[name] = w[:, o:o + size]
        o += size
    w_main = jnp.concatenate(
        [parts["z"], parts["xbc"], parts["q"] * (ATT_HEAD_DIM ** -0.5 * LOG2E), parts["v"],
         parts["g"], parts["gs"], parts["ga"]], axis=1).astype(BF16)
    w_kt = parts["k"].T.astype(BF16)
    w_dt = jnp.pad(parts["dt"], ((0, 0), (0, DT_PAD - SSM_HEADS))).astype(BF16)
    nw = norm_w[0].reshape(1, D_MODEL)
    cw = conv_w[0]
    cb = conv_b[0].reshape(1, CONV_DIM)
    dtb = jnp.pad(dt_bias[0], (0, DT_PAD - SSM_HEADS)).reshape(1, DT_PAD)
    alog = jnp.pad(a_log[0], (0, DT_PAD - SSM_HEADS)).reshape(1, DT_PAD)
    dskip_x = jnp.repeat(d_skip[0], SSM_HEAD_DIM).reshape(1, D_INNER)
    snw = ssm_norm_w[0].reshape(1, D_INNER)
    wos = w_out_ssm[0].astype(BF16)
    woa = w_out_att[0].astype(BF16)
    wo = w_o[0].astype(BF16)
    fnw = final_norm_w.reshape(1, D_MODEL)
    bias = _bias_table(_rel_bias_rows(rel_bias[0]))

    ssd_params = (cw, cb, dtb, alog, dskip_x, snw)
    out_params = (wos, woa, wo, fnw)

    xp = x_prompt.reshape(bp * lp, D_MODEL)
    p_p, dt_p, kt_p = _proj(xp, nw, w_main, w_dt, w_kt)
    h0_p = jnp.zeros((bp, D_INNER, D_STATE), F32)
    conv0_p = jnp.zeros((bp, 8, CONV_DIM), F32)
    yz_p, ssm_p, tail_p = _ssd(p_p, dt_p, h0_p, conv0_p, *ssd_params, nb=bp, nc=ncp)
    og_p, k_p, v_p = _attn_prompt(p_p, kt_p, bias, nb=bp, seq=lp)
    y_p = _out(yz_p, og_p, p_p, xp, *out_params).reshape(bp, lp, D_MODEL)
    keep = LEFT_CHUNKS * CHUNK
    k_p = k_p.reshape(1, bp, keep, ATT_HEADS, ATT_HEAD_DIM)
    v_p = v_p.reshape(1, bp, keep, ATT_HEADS, ATT_HEAD_DIM)

    xs = x_sample.reshape(bs * ls, D_MODEL)
    p_s, dt_s, kt_s = _proj(xs, nw, w_main, w_dt, w_kt)
    h0_s = state_ssm[0].reshape(bs, D_INNER, D_STATE)
    conv0_s = jnp.pad(state_conv[0], ((0, 0), (8 - (CONV_W - 1), 0), (0, 0)))
    yz_s, ssm_s, tail_s = _ssd(p_s, dt_s, h0_s, conv0_s, *ssd_params, nb=bs, nc=1)
    og_s, k_s, v_s = _attn_step(p_s, kt_s.T, cache_k[0].reshape(bs, keep, D_ATT),
                                cache_v[0].reshape(bs, keep, D_ATT), bias, nb=bs)
    y_s = _out(yz_s, og_s, p_s, xs, *out_params).reshape(bs, ls, D_MODEL)
    k_s = k_s.reshape(1, bs, keep, ATT_HEADS, ATT_HEAD_DIM)
    v_s = v_s.reshape(1, bs, keep, ATT_HEADS, ATT_HEAD_DIM)

    def states(ssm, tail, nb):
        return (ssm.reshape(1, nb, SSM_HEADS, SSM_HEAD_DIM, D_STATE),
                tail[:, 8 - (CONV_W - 1):, :].reshape(1, nb, CONV_W - 1, CONV_DIM))

    ssm_p5, conv_p4 = states(ssm_p, tail_p, bp)
    ssm_s5, conv_s4 = states(ssm_s, tail_s, bs)
    return (y_p, y_s, ssm_p5, conv_p4, k_p, v_p, ssm_s5, conv_s4, k_s, v_s)
```

```python
import functools

import jax
import jax.numpy as jnp
from jax import lax
from jax.experimental import pallas as pl
from jax.experimental.pallas import tpu as pltpu

D_MODEL = 1024
CHUNK = 64
D_INNER = 2048
SSM_HEAD_DIM = 64
SSM_HEADS = 32
N_GROUPS = 4
D_STATE = 128
GROUP_W = D_INNER // N_GROUPS
BC_W = N_GROUPS * D_STATE
CONV_W = 4
CONV_DIM = D_INNER + 2 * BC_W
ATT_HEADS = 16
ATT_HEAD_DIM = 64
D_ATT = 1024
LEFT_CHUNKS = 8
BAND_CHUNKS = LEFT_CHUNKS + 1
BAND = BAND_CHUNKS * CHUNK
REL_CLIP = 256
NORM_EPS = 1e-5
NEG_INF = -1e30
LOG2E = 1.4426950408889634

COL_Z = 0
COL_X = 2048
COL_B = 4096
COL_C = 4608
COL_Q = 5120
COL_V = 6144
COL_G = 7168
COL_GS = 8192
COL_GA = 9216
P_COLS = 10240
DT_PAD = 128

VMEM_LIMIT = 56 * 1024 * 1024

F32 = jnp.float32
BF16 = jnp.bfloat16


def _sigmoid(x):
    return 0.5 * jnp.tanh(0.5 * x) + 0.5


def _silu(x):
    h = 0.5 * x
    return h * jnp.tanh(h) + h


def _softplus(x):
    return jnp.maximum(x, 0.0) + jnp.log(1.0 + jnp.exp(-jnp.abs(x)))


PROJ_TM = 1024
PROJ_TN = 2560


def _proj_kernel(x_ref, nw_ref, w_ref, wdt_ref, wkt_ref, p_ref, dt_ref, kt_ref, xn_ref):
    @pl.when(pl.program_id(1) == 0)
    def _():
        x = x_ref[...]
        ms = jnp.mean(x * x, axis=-1, keepdims=True)
        xn = (x * lax.rsqrt(ms + NORM_EPS)) * nw_ref[...]
        xn_ref[...] = xn.astype(BF16)
        dt_ref[...] = jnp.dot(xn_ref[...], wdt_ref[...], preferred_element_type=F32)
        kt_ref[...] = lax.dot_general(wkt_ref[...], xn_ref[...], (((1,), (1,)), ((), ())),
                                      preferred_element_type=F32).astype(BF16)

    p_ref[...] = jnp.dot(xn_ref[...], w_ref[...], preferred_element_type=F32).astype(BF16)


def _proj(x2d, norm_w, w_main, w_dt, w_kt):
    n = x2d.shape[0]
    tm = min(PROJ_TM, n)
    return pl.pallas_call(
        _proj_kernel,
        grid=(n // tm, P_COLS // PROJ_TN),
        in_specs=[
            pl.BlockSpec((tm, D_MODEL), lambda i, j: (i, 0)),
            pl.BlockSpec((1, D_MODEL), lambda i, j: (0, 0)),
            pl.BlockSpec((D_MODEL, PROJ_TN), lambda i, j: (0, j)),
            pl.BlockSpec((D_MODEL, DT_PAD), lambda i, j: (0, 0)),
            pl.BlockSpec((D_ATT, D_MODEL), lambda i, j: (0, 0)),
        ],
        out_specs=[
            pl.BlockSpec((tm, PROJ_TN), lambda i, j: (i, j)),
            pl.BlockSpec((tm, DT_PAD), lambda i, j: (i, 0)),
            pl.BlockSpec((D_ATT, tm), lambda i, j: (0, i)),
        ],
        out_shape=[
            jax.ShapeDtypeStruct((n, P_COLS), BF16),
            jax.ShapeDtypeStruct((n, DT_PAD), F32),
            jax.ShapeDtypeStruct((D_ATT, n), BF16),
        ],
        scratch_shapes=[pltpu.VMEM((tm, D_MODEL), BF16)],
        compiler_params=pltpu.CompilerParams(
            dimension_semantics=("arbitrary", "arbitrary"),
            vmem_limit_bytes=VMEM_LIMIT),
        name="proj",
    )(x2d, norm_w, w_main, w_dt, w_kt)


SSD_SUB = 4


def _conv_silu(u_bf16, tail, w, b):
    u = u_bf16.astype(F32)
    row8 = lax.broadcasted_iota(jnp.int32, (8, 1), 0)
    acc = b + w[CONV_W - 1:CONV_W, :] * u
    for s in range(1, CONV_W):
        us = pltpu.roll(u, s, axis=0)
        ts = pltpu.roll(tail, s, axis=0)
        head = jnp.where(row8 < s, ts, us[0:8, :])
        us = jnp.concatenate([head, us[8:, :]], axis=0)
        acc = acc + w[CONV_W - 1 - s:CONV_W - s, :] * us
    return _silu(acc), u[CHUNK - 8:, :]


def _ssd_kernel(z_ref, x_ref, b_ref, c_ref, dtr_ref, h0_ref, conv0_ref,
                cw_ref, cb_ref, dtb_ref, alog_ref, dskip_ref, nw_ref,
                yz_ref, hout_ref, tailout_ref,
                ht_ref, tail_ref, xw_ref, y_ref, ea_ref, *, nsub):
    c = pl.program_id(1)
    n_chunks = pl.num_programs(1)

    @pl.when(c == 0)
    def _():
        for g in range(N_GROUPS):
            sl = slice(g * GROUP_W, (g + 1) * GROUP_W)
            ht_ref[:, sl] = h0_ref[0, sl, :].T
        tail_ref[...] = conv0_ref[0]

    def chunk(rows):
        xs, tail_x = _conv_silu(x_ref[rows, :], tail_ref[:, 0:D_INNER],
                                cw_ref[:, 0:D_INNER], cb_ref[:, 0:D_INNER])
        bm, tail_b = _conv_silu(b_ref[rows, :], tail_ref[:, D_INNER:D_INNER + BC_W],
                                cw_ref[:, D_INNER:D_INNER + BC_W], cb_ref[:, D_INNER:D_INNER + BC_W])
        cm, tail_c = _conv_silu(c_ref[rows, :], tail_ref[:, D_INNER + BC_W:],
                                cw_ref[:, D_INNER + BC_W:], cb_ref[:, D_INNER + BC_W:])
        tail_ref[:, 0:D_INNER] = tail_x
        tail_ref[:, D_INNER:D_INNER + BC_W] = tail_b
        tail_ref[:, D_INNER + BC_W:] = tail_c
        bm_b = bm.astype(BF16)
        cm_b = cm.astype(BF16)

        dt = _softplus(dtr_ref[rows, :] + dtb_ref[...])
        adt = dt * (-jnp.exp(alog_ref[...]))
        li = lax.broadcasted_iota(jnp.int32, (CHUNK, CHUNK), 0)
        si = lax.broadcasted_iota(jnp.int32, (CHUNK, CHUNK), 1)
        tri = (li >= si).astype(F32)
        acs = jnp.dot(tri, adt, precision=lax.Precision.HIGHEST,
                      preferred_element_type=F32)
        s2 = lax.broadcasted_iota(jnp.int32, (CHUNK, 2 * CHUNK), 0)
        l2 = lax.broadcasted_iota(jnp.int32, (CHUNK, 2 * CHUNK), 1) % CHUNK
        tri_t2 = (l2 >= s2).astype(F32)
        acs_t2 = lax.dot_general(adt, tri_t2, (((0,), (0,)), ((), ())),
                                 precision=lax.Precision.HIGHEST,
                                 preferred_element_type=F32)

        lane = lax.broadcasted_iota(jnp.int32, (CHUNK, 2 * CHUNK), 1)
        lo = lane < CHUNK
        lo_row = lo[0:1, :]
        tri2 = (lax.broadcasted_iota(jnp.int32, (CHUNK, 2 * CHUNK), 0) >= (lane % CHUNK))

        yoff = []
        cb2 = []
        for g in range(N_GROUPS):
            sl = slice(g * GROUP_W, (g + 1) * GROUP_W)
            ns = slice(g * D_STATE, (g + 1) * D_STATE)
            yoff.append(jnp.dot(cm_b[:, ns], ht_ref[:, sl].astype(BF16),
                                preferred_element_type=F32))
            b2 = jnp.concatenate([bm_b[:, ns], bm_b[:, ns]], axis=0)
            cb2.append(lax.dot_general(cm_b[:, ns], b2, (((1,), (1,)), ((), ())),
                                       preferred_element_type=F32))

        for j in range(SSM_HEADS // 2):
            g = j // (SSM_HEADS // 2 // N_GROUPS)
            r0, r1 = 2 * j, 2 * j + 1
            sl = slice(j * 128, (j + 1) * 128)
            gsl = slice((j % 4) * 128, (j % 4 + 1) * 128)
            u_col = jnp.where(lo, acs[:, r0:r0 + 1], acs[:, r1:r1 + 1])
            v_row = jnp.where(lo_row, acs_t2[r0:r0 + 1, :], acs_t2[r1:r1 + 1, :])
            dtp = jnp.where(lo, dt[:, r0:r0 + 1], dt[:, r1:r1 + 1])
            decay = jnp.exp(jnp.where(tri2, u_col - v_row, -jnp.inf))
            m = (cb2[g] * decay).astype(BF16)
            xs_p = xs[:, sl]
            xdt = xs_p * dtp
            xdt_b = xdt.astype(BF16)
            zero = jnp.zeros_like(xdt_b)
            rhs = jnp.concatenate([jnp.where(lo, xdt_b, zero), jnp.where(lo, zero, xdt_b)], axis=0)
            yd = jnp.dot(m, rhs, preferred_element_type=F32)
            e_u = jnp.exp(u_col)
            y = yd + yoff[g][:, gsl] * e_u + dskip_ref[:, sl] * xs_p
            to_end = jnp.exp(u_col[CHUNK - 1:CHUNK, :] - u_col)
            xw_ref[:, sl] = (xdt * to_end).astype(BF16)
            ea_ref[:, sl] = e_u[CHUNK - 8:CHUNK, :]
            zf = z_ref[rows, sl].astype(F32)
            y_ref[:, sl] = y * _silu(zf)

        for g in range(N_GROUPS):
            sl = slice(g * GROUP_W, (g + 1) * GROUP_W)
            ns = slice(g * D_STATE, (g + 1) * D_STATE)
            yg = y_ref[:, sl]
            ms = jnp.mean(yg * yg, axis=-1, keepdims=True)
            yz_ref[rows, sl] = ((yg * lax.rsqrt(ms + NORM_EPS)) * nw_ref[:, sl]).astype(BF16)
            upd = lax.dot_general(bm_b[:, ns], xw_ref[:, sl], (((0,), (0,)), ((), ())),
                                  preferred_element_type=F32)
            ht_ref[:, sl] = ht_ref[:, sl] * ea_ref[7:8, sl] + upd

    if nsub == 1:
        chunk(slice(None))
    else:
        def sub(i, carry):
            chunk(pl.ds(pl.multiple_of(i * CHUNK, CHUNK), CHUNK))
            return carry
        lax.fori_loop(0, nsub, sub, 0)

    @pl.when(c == n_chunks - 1)
    def _():
        for g in range(N_GROUPS):
            sl = slice(g * GROUP_W, (g + 1) * GROUP_W)
            hout_ref[0, sl, :] = ht_ref[:, sl].T
        tailout_ref[0] = tail_ref[...]


def _ssd(p, dtraw, h0, conv0, conv_w, conv_b, dt_bias, a_log, d_skip_x, ssm_norm_w, nb, nc):
    n = nb * nc * CHUNK
    nsub = SSD_SUB if nc % SSD_SUB == 0 else 1
    ns = nc // nsub
    tr = nsub * CHUNK
    row = lambda b, c: b * ns + c
    const = lambda b, c: (0, 0)
    return pl.pallas_call(
        functools.partial(_ssd_kernel, nsub=nsub),
        grid=(nb, ns),
        in_specs=[
            pl.BlockSpec((tr, D_INNER), lambda b, c: (row(b, c), COL_Z // D_INNER)),
            pl.BlockSpec((tr, D_INNER), lambda b, c: (row(b, c), COL_X // D_INNER)),
            pl.BlockSpec((tr, BC_W), lambda b, c: (row(b, c), COL_B // BC_W)),
            pl.BlockSpec((tr, BC_W), lambda b, c: (row(b, c), COL_C // BC_W)),
            pl.BlockSpec((tr, DT_PAD), lambda b, c: (row(b, c), 0)),
            pl.BlockSpec((1, D_INNER, D_STATE), lambda b, c: (b, 0, 0)),
            pl.BlockSpec((1, 8, CONV_DIM), lambda b, c: (b, 0, 0)),
            pl.BlockSpec((CONV_W, CONV_DIM), const),
            pl.BlockSpec((1, CONV_DIM), const),
            pl.BlockSpec((1, DT_PAD), const),
            pl.BlockSpec((1, DT_PAD), const),
            pl.BlockSpec((1, D_INNER), const),
            pl.BlockSpec((1, D_INNER), const),
        ],
        out_specs=[
            pl.BlockSpec((tr, D_INNER), lambda b, c: (row(b, c), 0)),
            pl.BlockSpec((1, D_INNER, D_STATE), lambda b, c: (b, 0, 0)),
            pl.BlockSpec((1, 8, CONV_DIM), lambda b, c: (b, 0, 0)),
        ],
        out_shape=[
            jax.ShapeDtypeStruct((n, D_INNER), BF16),
            jax.ShapeDtypeStruct((nb, D_INNER, D_STATE), F32),
            jax.ShapeDtypeStruct((nb, 8, CONV_DIM), F32),
        ],
        scratch_shapes=[
            pltpu.VMEM((D_STATE, D_INNER), F32),
            pltpu.VMEM((8, CONV_DIM), F32),
            pltpu.VMEM((CHUNK, D_INNER), BF16),
            pltpu.VMEM((CHUNK, D_INNER), F32),
            pltpu.VMEM((8, D_INNER), F32),
        ],
        compiler_params=pltpu.CompilerParams(
            dimension_semantics=("arbitrary", "arbitrary"),
            vmem_limit_bytes=VMEM_LIMIT),
        name="ssd",
    )(p, p, p, p, dtraw, h0, conv0, conv_w, conv_b, dt_bias, a_log, d_skip_x, ssm_norm_w)


ATT_TQ = 256
ATT_TILES = 3
ATT_KEYS = ATT_TILES * ATT_TQ
BIAS_ROW = 1024
KEEP_TILES = LEFT_CHUNKS * CHUNK // ATT_TQ


def _bias_kernel(r_ref, o_ref):
    x = jnp.broadcast_to(r_ref[0], (ATT_TQ, BIAS_ROW))
    tab = pltpu.roll(x, BIAS_ROW - ATT_TQ, axis=1, stride=1, stride_axis=0)[:, :ATT_KEYS]
    qc = lax.broadcasted_iota(jnp.int32, (ATT_TQ, ATT_KEYS), 0) // CHUNK
    kc = lax.broadcasted_iota(jnp.int32, (ATT_TQ, ATT_KEYS), 1) // CHUNK
    ok = (kc >= qc) & (kc <= qc + LEFT_CHUNKS)
    o_ref[0] = jnp.where(ok, tab * LOG2E, NEG_INF)


def _bias_table(rel_row):
    return pl.pallas_call(
        _bias_kernel,
        grid=(ATT_HEADS,),
        in_specs=[pl.BlockSpec((1, 1, BIAS_ROW), lambda h: (h, 0, 0))],
        out_specs=pl.BlockSpec((1, ATT_TQ, ATT_KEYS), lambda h: (h, 0, 0)),
        out_shape=jax.ShapeDtypeStruct((ATT_HEADS, ATT_TQ, ATT_KEYS), F32),
        name="bias_table",
    )(rel_row)


def _softmax_pv(s_tiles, v_tiles):
    def lane_reduce(tiles, combine, reduce):
        if len({t.shape for t in tiles}) == 1:
            tiles = [functools.reduce(combine, tiles)]
        return functools.reduce(combine, [reduce(t, axis=-1, keepdims=True) for t in tiles])

    mx = lane_reduce(s_tiles, jnp.maximum, jnp.max)
    p_tiles = [jnp.exp2(s - mx) for s in s_tiles]
    den = lane_reduce(p_tiles, jnp.add, jnp.sum)
    o = functools.reduce(jnp.add, [jnp.dot(p.astype(BF16), v, preferred_element_type=F32)
                                   for p, v in zip(p_tiles, v_tiles)])
    return o / den


def _attn_prompt_kernel(q_ref, g_ref, kt0_ref, kt1_ref, kt2_ref, v0_ref, v1_ref, v2_ref,
                        bias_ref, o_ref, kp_ref, vp_ref):
    t = pl.program_id(1)
    kt_refs = (kt0_ref, kt1_ref, kt2_ref)
    v_refs = (v0_ref, v1_ref, v2_ref)

    @pl.when(t >= pl.num_programs(1) - KEEP_TILES)
    def _():
        kp_ref[0] = kt2_ref[...].astype(F32).T
        vp_ref[0] = v2_ref[...].astype(F32)

    def body(mask_start):
        lo = lax.broadcasted_iota(jnp.int32, (ATT_TQ, 2 * ATT_HEAD_DIM), 1) < ATT_HEAD_DIM
        for hp in range(ATT_HEADS // 2):
            sl = slice(hp * 128, (hp + 1) * 128)
            qp = q_ref[:, sl]
            zero = jnp.zeros_like(qp)
            q2 = jnp.concatenate([jnp.where(lo, qp, zero), jnp.where(lo, zero, qp)], axis=0)
            s_tiles = []
            for n in range(ATT_TILES):
                ks = slice(n * ATT_TQ, (n + 1) * ATT_TQ)
                s = jnp.dot(q2, kt_refs[n][sl, :], preferred_element_type=F32)
                s = s + jnp.concatenate([bias_ref[2 * hp, :, ks], bias_ref[2 * hp + 1, :, ks]],
                                        axis=0)
                if mask_start and n < ATT_TILES - 1:
                    s = jnp.where(t + n >= ATT_TILES - 1, s, NEG_INF)
                s_tiles.append(s)
            o2 = _softmax_pv(s_tiles, [v_refs[n][:, sl] for n in range(ATT_TILES)])
            o_pair = jnp.where(lo, o2[:ATT_TQ], o2[ATT_TQ:])
            gate = g_ref[:, sl].astype(F32)
            o_ref[:, sl] = (o_pair * _silu(gate)).astype(BF16)

    pl.when(t < ATT_TILES - 1)(functools.partial(body, True))
    pl.when(t >= ATT_TILES - 1)(functools.partial(body, False))


def _attn_prompt(p, kt, bias, nb, seq):
    nt = seq // ATT_TQ

    def tile_row(n):
        return lambda b, t: b * nt + jnp.maximum(t - (ATT_TILES - 1) + n, 0)

    in_specs = [
        pl.BlockSpec((ATT_TQ, D_ATT), lambda b, t: (b * nt + t, COL_Q // D_ATT)),
        pl.BlockSpec((ATT_TQ, D_ATT), lambda b, t: (b * nt + t, COL_G // D_ATT)),
    ]
    in_specs += [pl.BlockSpec((D_ATT, ATT_TQ), lambda b, t, r=tile_row(n): (0, r(b, t)))
                 for n in range(ATT_TILES)]
    in_specs += [pl.BlockSpec((ATT_TQ, D_ATT), lambda b, t, r=tile_row(n): (r(b, t), COL_V // D_ATT))
                 for n in range(ATT_TILES)]
    in_specs += [pl.BlockSpec((ATT_HEADS, ATT_TQ, ATT_KEYS), lambda b, t: (0, 0, 0),
                              pipeline_mode=pl.Buffered(1))]
    keep_spec = pl.BlockSpec((1, ATT_TQ, D_ATT),
                             lambda b, t: (b, jnp.maximum(t - (nt - KEEP_TILES), 0), 0))
    keep_shape = jax.ShapeDtypeStruct((nb, KEEP_TILES * ATT_TQ, D_ATT), F32)
    return pl.pallas_call(
        _attn_prompt_kernel,
        grid=(nb, nt),
        in_specs=in_specs,
        out_specs=[pl.BlockSpec((ATT_TQ, D_ATT), lambda b, t: (b * nt + t, 0)),
                   keep_spec, keep_spec],
        out_shape=[jax.ShapeDtypeStruct((nb * seq, D_ATT), BF16), keep_shape, keep_shape],
        compiler_params=pltpu.CompilerParams(
            dimension_semantics=("arbitrary", "arbitrary"),
            vmem_limit_bytes=VMEM_LIMIT),
        name="attn_prompt",
    )(p, p, kt, kt, kt, p, p, p, bias)


def _attn_step_kernel(q_ref, g_ref, kn_ref, vn_ref, kc_ref, vc_ref, bias_ref,
                      o_ref, ks_ref, vs_ref):
    keep = LEFT_CHUNKS * CHUNK
    ks_ref[0, 0:keep - CHUNK, :] = kc_ref[0, CHUNK:keep, :]
    ks_ref[0, keep - CHUNK:keep, :] = kn_ref[...].astype(F32)
    vs_ref[0, 0:keep - CHUNK, :] = vc_ref[0, CHUNK:keep, :]
    vs_ref[0, keep - CHUNK:keep, :] = vn_ref[...].astype(F32)
    lo = lax.broadcasted_iota(jnp.int32, (CHUNK, 2 * ATT_HEAD_DIM), 1) < ATT_HEAD_DIM

    for hp in range(ATT_HEADS // 2):
        sl = slice(hp * 128, (hp + 1) * 128)
        qp = q_ref[:, sl]
        k_tiles = [kc_ref[0, :, sl].astype(BF16), kn_ref[:, sl]]
        v_tiles = [vc_ref[0, :, sl].astype(BF16), vn_ref[:, sl]]
        zero = jnp.zeros_like(qp)
        outs = []
        for j in range(2):
            h = 2 * hp + j
            qm = jnp.where(lo, qp, zero) if j == 0 else jnp.where(lo, zero, qp)
            s_tiles = []
            for n, (k0, k1) in enumerate(((0, keep), (keep, BAND))):
                s = lax.dot_general(qm, k_tiles[n], (((1,), (1,)), ((), ())),
                                    preferred_element_type=F32)
                s_tiles.append(s + bias_ref[h, 0:CHUNK, k0:k1])
            outs.append(_softmax_pv(s_tiles, v_tiles))
        o_pair = jnp.where(lo, outs[0], outs[1])
        gate = g_ref[:, sl].astype(F32)
        o_ref[:, sl] = (o_pair * _silu(gate)).astype(BF16)


def _attn_step(p, k_new, cache_k, cache_v, bias, nb):
    keep = LEFT_CHUNKS * CHUNK
    new = lambda col: pl.BlockSpec((CHUNK, D_ATT), lambda b: (b, col))
    cache = pl.BlockSpec((1, keep, D_ATT), lambda b: (b, 0, 0))
    return pl.pallas_call(
        _attn_step_kernel,
        grid=(nb,),
        in_specs=[new(COL_Q // D_ATT), new(COL_G // D_ATT), new(0),
                  new(COL_V // D_ATT), cache, cache,
                  pl.BlockSpec((ATT_HEADS, ATT_TQ, ATT_KEYS), lambda b: (0, 0, 0),
                               pipeline_mode=pl.Buffered(1))],
        out_specs=[pl.BlockSpec((CHUNK, D_ATT), lambda b: (b, 0)), cache, cache],
        out_shape=[jax.ShapeDtypeStruct((nb * CHUNK, D_ATT), BF16),
                   jax.ShapeDtypeStruct((nb, keep, D_ATT), F32),
                   jax.ShapeDtypeStruct((nb, keep, D_ATT), F32)],
        compiler_params=pltpu.CompilerParams(
            dimension_semantics=("arbitrary",),
            vmem_limit_bytes=VMEM_LIMIT),
        name="attn_step",
    )(p, p, k_new, p, cache_k, cache_v, bias)


OUT_TM = 512


def _out_kernel(yz_ref, og_ref, gs_ref, ga_ref, x_ref, wos_ref, woa_ref, wo_ref, fnw_ref, y_ref):
    y_ssm = jnp.dot(yz_ref[...], wos_ref[...], preferred_element_type=F32)
    y_att = jnp.dot(og_ref[...], woa_ref[...], preferred_element_type=F32)
    merged = (_sigmoid(gs_ref[...].astype(F32)) * y_ssm
              + _sigmoid(ga_ref[...].astype(F32)) * y_att)
    h = x_ref[...] + jnp.dot(merged.astype(BF16), wo_ref[...], preferred_element_type=F32)
    ms = jnp.mean(h * h, axis=-1, keepdims=True)
    y_ref[...] = (h * lax.rsqrt(ms + NORM_EPS)) * fnw_ref[...]


def _out(yz, og, p, x2d, w_out_ssm, w_out_att, w_o, final_norm_w):
    n = x2d.shape[0]
    tm = min(OUT_TM, n)
    const = lambda i: (0, 0)
    return pl.pallas_call(
        _out_kernel,
        grid=(n // tm,),
        in_specs=[
            pl.BlockSpec((tm, D_INNER), lambda i: (i, 0)),
            pl.BlockSpec((tm, D_ATT), lambda i: (i, 0)),
            pl.BlockSpec((tm, D_MODEL), lambda i: (i, COL_GS // D_MODEL)),
            pl.BlockSpec((tm, D_MODEL), lambda i: (i, COL_GA // D_MODEL)),
            pl.BlockSpec((tm, D_MODEL), lambda i: (i, 0)),
            pl.BlockSpec((D_INNER, D_MODEL), const),
            pl.BlockSpec((D_ATT, D_MODEL), const),
            pl.BlockSpec((D_MODEL, D_MODEL), const),
            pl.BlockSpec((1, D_MODEL), const),
        ],
        out_specs=pl.BlockSpec((tm, D_MODEL), lambda i: (i, 0)),
        out_shape=jax.ShapeDtypeStruct((n, D_MODEL), F32),
        compiler_params=pltpu.CompilerParams(
            dimension_semantics=("arbitrary",),
            vmem_limit_bytes=VMEM_LIMIT),
        name="out",
    )(yz, og, p, p, x2d, w_out_ssm, w_out_att, w_o, final_norm_w)


def _rel_bias_rows(rel_bias):
    d = ATT_TILES * ATT_TQ - jnp.arange(BIAS_ROW)
    idx = jnp.clip(d, -REL_CLIP, REL_CLIP) + REL_CLIP
    return rel_bias[:, idx].astype(F32).reshape(ATT_HEADS, 1, BIAS_ROW)


def kernel(x_prompt, x_sample, state_ssm, state_conv, cache_k, cache_v, norm_w, w_in, conv_w,
           conv_b, dt_bias, a_log, d_skip, ssm_norm_w, w_out_ssm, rel_bias, w_out_att, w_o,
           final_norm_w):
    bp, lp, _ = x_prompt.shape
    bs, ls, _ = x_sample.shape
    ncp = lp // CHUNK
    assert ls == CHUNK and cache_k.shape[2] == LEFT_CHUNKS * CHUNK

    w = w_in[0]
    o = 0
    parts = {}
    for name, size in (("z", D_INNER), ("xbc", CONV_DIM), ("dt", SSM_HEADS), ("q", D_ATT),
                       ("k", D_ATT), ("v", D_ATT), ("g", D_ATT), ("gs", D_MODEL), ("ga", D_MODEL)):
        parts[name] = w[:, o:o + size]
        o += size
    w_main = jnp.concatenate(
        [parts["z"], parts["xbc"], parts["q"] * (ATT_HEAD_DIM ** -0.5 * LOG2E), parts["v"],
         parts["g"], parts["gs"], parts["ga"]], axis=1).astype(BF16)
    w_kt = parts["k"].T.astype(BF16)
    w_dt = jnp.pad(parts["dt"], ((0, 0), (0, DT_PAD - SSM_HEADS))).astype(BF16)
    nw = norm_w[0].reshape(1, D_MODEL)
    cw = conv_w[0]
    cb = conv_b[0].reshape(1, CONV_DIM)
    dtb = jnp.pad(dt_bias[0], (0, DT_PAD - SSM_HEADS)).reshape(1, DT_PAD)
    alog = jnp.pad(a_log[0], (0, DT_PAD - SSM_HEADS)).reshape(1, DT_PAD)
    dskip_x = jnp.repeat(d_skip[0], SSM_HEAD_DIM).reshape(1, D_INNER)
    snw = ssm_norm_w[0].reshape(1, D_INNER)
    wos = w_out_ssm[0].astype(BF16)
    woa = w_out_att[0].astype(BF16)
    wo = w_o[0].astype(BF16)
    fnw = final_norm_w.reshape(1, D_MODEL)
    bias = _bias_table(_rel_bias_rows(rel_bias[0]))

    ssd_params = (cw, cb, dtb, alog, dskip_x, snw)
    out_params = (wos, woa, wo, fnw)

    xp = x_prompt.reshape(bp * lp, D_MODEL)
    p_p, dt_p, kt_p = _proj(xp, nw, w_main, w_dt, w_kt)
    h0_p = jnp.zeros((bp, D_INNER, D_STATE), F32)
    conv0_p = jnp.zeros((bp, 8, CONV_DIM), F32)
    yz_p, ssm_p, tail_p = _ssd(p_p, dt_p, h0_p, conv0_p, *ssd_params, nb=bp, nc=ncp)
    og_p, k_p, v_p = _attn_prompt(p_p, kt_p, bias, nb=bp, seq=lp)
    y_p = _out(yz_p, og_p, p_p, xp, *out_params).reshape(bp, lp, D_MODEL)
    keep = LEFT_CHUNKS * CHUNK
    k_p = k_p.reshape(1, bp, keep, ATT_HEADS, ATT_HEAD_DIM)
    v_p = v_p.reshape(1, bp, keep, ATT_HEADS, ATT_HEAD_DIM)

    xs = x_sample.reshape(bs * ls, D_MODEL)
    p_s, dt_s, kt_s = _proj(xs, nw, w_main, w_dt, w_kt)
    h0_s = state_ssm[0].reshape(bs, D_INNER, D_STATE)
    conv0_s = jnp.pad(state_conv[0], ((0, 0), (8 - (CONV_W - 1), 0), (0, 0)))
    yz_s, ssm_s, tail_s = _ssd(p_s, dt_s, h0_s, conv0_s, *ssd_params, nb=bs, nc=1)
    og_s, k_s, v_s = _attn_step(p_s, kt_s.T, cache_k[0].reshape(bs, keep, D_ATT),
                                cache_v[0].reshape(bs, keep, D_ATT), bias, nb=bs)
    y_s = _out(yz_s, og_s, p_s, xs, *out_params).reshape(bs, ls, D_MODEL)
    k_s = k_s.reshape(1, bs, keep, ATT_HEADS, ATT_HEAD_DIM)
    v_s = v_s.reshape(1, bs, keep, ATT_HEADS, ATT_HEAD_DIM)

    def states(ssm, tail, nb):
        return (ssm.reshape(1, nb, SSM_HEADS, SSM_HEAD_DIM, D_STATE),
                tail[:, 8 - (CONV_W - 1):, :].reshape(1, nb, CONV_W - 1, CONV_DIM))

    ssm_p5, conv_p4 = states(ssm_p, tail_p, bp)
    ssm_s5, conv_s4 = states(ssm_s, tail_s, bs)
    return (y_p, y_s, ssm_p5, conv_p4, k_p, v_p, ssm_s5, conv_s4, k_s, v_s)
```

```python
import functools

import jax
import jax.numpy as jnp
from jax import lax
from jax.experimental import pallas as pl
from jax.experimental.pallas import tpu as pltpu

D_MODEL = 1024
CHUNK = 64
D_INNER = 2048
SSM_HEAD_DIM = 64
SSM_HEADS = 32
N_GROUPS = 4
D_STATE = 128
GROUP_W = D_INNER // N_GROUPS
BC_W = N_GROUPS * D_STATE
CONV_W = 4
CONV_DIM = D_INNER + 2 * BC_W
ATT_HEADS = 16
ATT_HEAD_DIM = 64
D_ATT = 1024
LEFT_CHUNKS = 8
BAND_CHUNKS = LEFT_CHUNKS + 1
BAND = BAND_CHUNKS * CHUNK
REL_CLIP = 256
NORM_EPS = 1e-5
NEG_INF = -1e30
LOG2E = 1.4426950408889634

COL_Z = 0
COL_X = 2048
COL_B = 4096
COL_C = 4608
COL_Q = 5120
COL_V = 6144
COL_G = 7168
COL_GS = 8192
COL_GA = 9216
P_COLS = 10240
DT_PAD = 128

VMEM_LIMIT = 56 * 1024 * 1024

F32 = jnp.float32
BF16 = jnp.bfloat16


def _sigmoid(x):
    return 0.5 * jnp.tanh(0.5 * x) + 0.5


def _silu(x):
    h = 0.5 * x
    return h * jnp.tanh(h) + h


def _softplus(x):
    return jnp.maximum(x, 0.0) + jnp.log(1.0 + jnp.exp(-jnp.abs(x)))


PROJ_TM = 1024
PROJ_TN = 2560


def _proj_kernel(x_ref, nw_ref, w_ref, wdt_ref, wkt_ref, p_ref, dt_ref, kt_ref, xn_ref):
    @pl.when(pl.program_id(1) == 0)
    def _():
        x = x_ref[...]
        ms = jnp.mean(x * x, axis=-1, keepdims=True)
        xn = (x * lax.rsqrt(ms + NORM_EPS)) * nw_ref[...]
        xn_ref[...] = xn.astype(BF16)
        dt_ref[...] = jnp.dot(xn_ref[...], wdt_ref[...], preferred_element_type=F32)
        kt_ref[...] = lax.dot_general(wkt_ref[...], xn_ref[...], (((1,), (1,)), ((), ())),
                                      preferred_element_type=F32).astype(BF16)

    p_ref[...] = jnp.dot(xn_ref[...], w_ref[...], preferred_element_type=F32).astype(BF16)


def _proj(x2d, norm_w, w_main, w_dt, w_kt):
    n = x2d.shape[0]
    tm = min(PROJ_TM, n)
    return pl.pallas_call(
        _proj_kernel,
        grid=(n // tm, P_COLS // PROJ_TN),
        in_specs=[
            pl.BlockSpec((tm, D_MODEL), lambda i, j: (i, 0)),
            pl.BlockSpec((1, D_MODEL), lambda i, j: (0, 0)),
            pl.BlockSpec((D_MODEL, PROJ_TN), lambda i, j: (0, j)),
            pl.BlockSpec((D_MODEL, DT_PAD), lambda i, j: (0, 0)),
            pl.BlockSpec((D_ATT, D_MODEL), lambda i, j: (0, 0)),
        ],
        out_specs=[
            pl.BlockSpec((tm, PROJ_TN), lambda i, j: (i, j)),
            pl.BlockSpec((tm, DT_PAD), lambda i, j: (i, 0)),
            pl.BlockSpec((D_ATT, tm), lambda i, j: (0, i)),
        ],
        out_shape=[
            jax.ShapeDtypeStruct((n, P_COLS), BF16),
            jax.ShapeDtypeStruct((n, DT_PAD), F32),
            jax.ShapeDtypeStruct((D_ATT, n), BF16),
        ],
        scratch_shapes=[pltpu.VMEM((tm, D_MODEL), BF16)],
        compiler_params=pltpu.CompilerParams(
            dimension_semantics=("arbitrary", "arbitrary"),
            vmem_limit_bytes=VMEM_LIMIT),
        name="proj",
    )(x2d, norm_w, w_main, w_dt, w_kt)


SSD_SUB = 4


def _shift_matrix():
    t = lax.broadcasted_iota(jnp.int32, (CHUNK, (CONV_W - 1) * CHUNK), 0)
    c = lax.broadcasted_iota(jnp.int32, (CHUNK, (CONV_W - 1) * CHUNK), 1)
    return (c % CHUNK == t - (c // CHUNK + 1)).astype(BF16)


def _conv_silu(u_bf16, tail_ref, cols, w, b, shift_mat):
    u = u_bf16.astype(F32)
    taps = [w[CONV_W - 1 - s:CONV_W - s, :] for s in range(CONV_W)]
    delayed = jnp.concatenate([(taps[s] * u).astype(BF16) for s in range(1, CONV_W)], axis=0)
    acc = b + taps[0] * u + jnp.dot(shift_mat, delayed, preferred_element_type=F32)
    tail = tail_ref[:, cols]
    row8 = lax.broadcasted_iota(jnp.int32, (8, 1), 0)
    head = jnp.zeros_like(tail)
    for s in range(1, CONV_W):
        head = head + jnp.where(row8 < s, taps[s] * pltpu.roll(tail, s, axis=0), 0.0)
    acc = jnp.concatenate([acc[0:8, :] + head, acc[8:, :]], axis=0)
    tail_ref[:, cols] = u[CHUNK - 8:, :]
    return _silu(acc)


def _ssd_kernel(z_ref, x_ref, b_ref, c_ref, dtr_ref, h0_ref, conv0_ref,
                cw_ref, cb_ref, dtb_ref, alog_ref, dskip_ref, nw_ref,
                yz_ref, hout_ref, tailout_ref,
                ht_ref, tail_ref, xw_ref, y_ref, ea_ref, *, nsub):
    c = pl.program_id(1)
    n_chunks = pl.num_programs(1)

    @pl.when(c == 0)
    def _():
        for g in range(N_GROUPS):
            sl = slice(g * GROUP_W, (g + 1) * GROUP_W)
            ht_ref[:, sl] = h0_ref[0, sl, :].T
        tail_ref[...] = conv0_ref[0]

    def chunk(rows):
        cx = slice(0, D_INNER)
        cbm = slice(D_INNER, D_INNER + BC_W)
        ccm = slice(D_INNER + BC_W, CONV_DIM)
        shift_mat = _shift_matrix()
        xs = _conv_silu(x_ref[rows, :], tail_ref, cx, cw_ref[:, cx], cb_ref[:, cx], shift_mat)
        bm = _conv_silu(b_ref[rows, :], tail_ref, cbm, cw_ref[:, cbm], cb_ref[:, cbm], shift_mat)
        cm = _conv_silu(c_ref[rows, :], tail_ref, ccm, cw_ref[:, ccm], cb_ref[:, ccm], shift_mat)
        bm_b = bm.astype(BF16)
        cm_b = cm.astype(BF16)

        dt = _softplus(dtr_ref[rows, :] + dtb_ref[...])
        adt = dt * (-jnp.exp(alog_ref[...]) * LOG2E)
        li = lax.broadcasted_iota(jnp.int32, (CHUNK, CHUNK), 0)
        si = lax.broadcasted_iota(jnp.int32, (CHUNK, CHUNK), 1)
        tri = (li >= si).astype(F32)
        acs = jnp.dot(tri, adt, precision=lax.Precision.HIGHEST,
                      preferred_element_type=F32)
        s2 = lax.broadcasted_iota(jnp.int32, (CHUNK, 2 * CHUNK), 0)
        l2 = lax.broadcasted_iota(jnp.int32, (CHUNK, 2 * CHUNK), 1) % CHUNK
        tri_t2 = (l2 >= s2).astype(F32)
        acs_t2 = lax.dot_general(adt, tri_t2, (((0,), (0,)), ((), ())),
                                 precision=lax.Precision.HIGHEST,
                                 preferred_element_type=F32)

        lane = lax.broadcasted_iota(jnp.int32, (CHUNK, 2 * CHUNK), 1)
        lo = lane < CHUNK
        lo_row = lo[0:1, :]
        tri2 = (lax.broadcasted_iota(jnp.int32, (CHUNK, 2 * CHUNK), 0) >= (lane % CHUNK))

        yoff = []
        cb2 = []
        for g in range(N_GROUPS):
            sl = slice(g * GROUP_W, (g + 1) * GROUP_W)
            ns = slice(g * D_STATE, (g + 1) * D_STATE)
            yoff.append(jnp.dot(cm_b[:, ns], ht_ref[:, sl].astype(BF16),
                                preferred_element_type=F32))
            b2 = jnp.concatenate([bm_b[:, ns], bm_b[:, ns]], axis=0)
            cb2.append(lax.dot_general(cm_b[:, ns], b2, (((1,), (1,)), ((), ())),
                                       preferred_element_type=F32))

        for j in range(SSM_HEADS // 2):
            g = j // (SSM_HEADS // 2 // N_GROUPS)
            r0, r1 = 2 * j, 2 * j + 1
            sl = slice(j * 128, (j + 1) * 128)
            gsl = slice((j % 4) * 128, (j % 4 + 1) * 128)
            u_col = jnp.where(lo, acs[:, r0:r0 + 1], acs[:, r1:r1 + 1])
            v_row = jnp.where(lo_row, acs_t2[r0:r0 + 1, :], acs_t2[r1:r1 + 1, :])
            dtp = jnp.where(lo, dt[:, r0:r0 + 1], dt[:, r1:r1 + 1])
            decay = jnp.exp2(jnp.where(tri2, u_col - v_row, -jnp.inf))
            m = (cb2[g] * decay).astype(BF16)
            xs_p = xs[:, sl]
            xdt = xs_p * dtp
            xdt_b = xdt.astype(BF16)
            zero = jnp.zeros_like(xdt_b)
            rhs = jnp.concatenate([jnp.where(lo, xdt_b, zero), jnp.where(lo, zero, xdt_b)], axis=0)
            yd = jnp.dot(m, rhs, preferred_element_type=F32)
            e_u = jnp.exp2(u_col)
            y = yd + yoff[g][:, gsl] * e_u + dskip_ref[:, sl] * xs_p
            to_end = jnp.exp2(u_col[CHUNK - 1:CHUNK, :] - u_col)
            xw_ref[:, sl] = (xdt * to_end).astype(BF16)
            ea_ref[:, sl] = e_u[CHUNK - 8:CHUNK, :]
            zf = z_ref[rows, sl].astype(F32)
            y_ref[:, sl] = y * _silu(zf)

        for g in range(N_GROUPS):
            sl = slice(g * GROUP_W, (g + 1) * GROUP_W)
            ns = slice(g * D_STATE, (g + 1) * D_STATE)
            yg = y_ref[:, sl]
            ms = jnp.mean(yg * yg, axis=-1, keepdims=True)
            yz_ref[rows, sl] = ((yg * lax.rsqrt(ms + NORM_EPS)) * nw_ref[:, sl]).astype(BF16)
            upd = lax.dot_general(bm_b[:, ns], xw_ref[:, sl], (((0,), (0,)), ((), ())),
                                  preferred_element_type=F32)
            ht_ref[:, sl] = ht_ref[:, sl] * ea_ref[7:8, sl] + upd

    if nsub == 1:
        chunk(slice(None))
    else:
        def sub(i, carry):
            chunk(pl.ds(pl.multiple_of(i * CHUNK, CHUNK), CHUNK))
            return carry
        lax.fori_loop(0, nsub, sub, 0)

    @pl.when(c == n_chunks - 1)
    def _():
        for g in range(N_GROUPS):
            sl = slice(g * GROUP_W, (g + 1) * GROUP_W)
            hout_ref[0, sl, :] = ht_ref[:, sl].T
        tailout_ref[0] = tail_ref[...]


def _ssd(p, dtraw, h0, conv0, conv_w, conv_b, dt_bias, a_log, d_skip_x, ssm_norm_w, nb, nc):
    n = nb * nc * CHUNK
    nsub = SSD_SUB if nc % SSD_SUB == 0 else 1
    ns = nc // nsub
    tr = nsub * CHUNK
    row = lambda b, c: b * ns + c
    const = lambda b, c: (0, 0)
    return pl.pallas_call(
        functools.partial(_ssd_kernel, nsub=nsub),
        grid=(nb, ns),
        in_specs=[
            pl.BlockSpec((tr, D_INNER), lambda b, c: (row(b, c), COL_Z // D_INNER)),
            pl.BlockSpec((tr, D_INNER), lambda b, c: (row(b, c), COL_X // D_INNER)),
            pl.BlockSpec((tr, BC_W), lambda b, c: (row(b, c), COL_B // BC_W)),
            pl.BlockSpec((tr, BC_W), lambda b, c: (row(b, c), COL_C // BC_W)),
            pl.BlockSpec((tr, DT_PAD), lambda b, c: (row(b, c), 0)),
            pl.BlockSpec((1, D_INNER, D_STATE), lambda b, c: (b, 0, 0)),
            pl.BlockSpec((1, 8, CONV_DIM), lambda b, c: (b, 0, 0)),
            pl.BlockSpec((CONV_W, CONV_DIM), const),
            pl.BlockSpec((1, CONV_DIM), const),
            pl.BlockSpec((1, DT_PAD), const),
            pl.BlockSpec((1, DT_PAD), const),
            pl.BlockSpec((1, D_INNER), const),
            pl.BlockSpec((1, D_INNER), const),
        ],
        out_specs=[
            pl.BlockSpec((tr, D_INNER), lambda b, c: (row(b, c), 0)),
            pl.BlockSpec((1, D_INNER, D_STATE), lambda b, c: (b, 0, 0)),
            pl.BlockSpec((1, 8, CONV_DIM), lambda b, c: (b, 0, 0)),
        ],
        out_shape=[
            jax.ShapeDtypeStruct((n, D_INNER), BF16),
            jax.ShapeDtypeStruct((nb, D_INNER, D_STATE), F32),
            jax.ShapeDtypeStruct((nb, 8, CONV_DIM), F32),
        ],
        scratch_shapes=[
            pltpu.VMEM((D_STATE, D_INNER), F32),
            pltpu.VMEM((8, CONV_DIM), F32),
            pltpu.VMEM((CHUNK, D_INNER), BF16),
            pltpu.VMEM((CHUNK, D_INNER), F32),
            pltpu.VMEM((8, D_INNER), F32),
        ],
        compiler_params=pltpu.CompilerParams(
            dimension_semantics=("arbitrary", "arbitrary"),
            vmem_limit_bytes=VMEM_LIMIT),
        name="ssd",
    )(p, p, p, p, dtraw, h0, conv0, conv_w, conv_b, dt_bias, a_log, d_skip_x, ssm_norm_w)


ATT_TQ = 256
ATT_TILES = 3
ATT_KEYS = ATT_TILES * ATT_TQ
BIAS_ROW = 1024
KEEP_TILES = LEFT_CHUNKS * CHUNK // ATT_TQ


def _bias_kernel(r_ref, o_ref):
    x = jnp.broadcast_to(r_ref[0], (ATT_TQ, BIAS_ROW))
    tab = pltpu.roll(x, BIAS_ROW - ATT_TQ, axis=1, stride=1, stride_axis=0)[:, :ATT_KEYS]
    qc = lax.broadcasted_iota(jnp.int32, (ATT_TQ, ATT_KEYS), 0) // CHUNK
    kc = lax.broadcasted_iota(jnp.int32, (ATT_TQ, ATT_KEYS), 1) // CHUNK
    ok = (kc >= qc) & (kc <= qc + LEFT_CHUNKS)
    o_ref[0] = jnp.where(ok, tab * LOG2E, NEG_INF)


def _bias_table(rel_row):
    return pl.pallas_call(
        _bias_kernel,
        grid=(ATT_HEADS,),
        in_specs=[pl.BlockSpec((1, 1, BIAS_ROW), lambda h: (h, 0, 0))],
        out_specs=pl.BlockSpec((1, ATT_TQ, ATT_KEYS), lambda h: (h, 0, 0)),
        out_shape=jax.ShapeDtypeStruct((ATT_HEADS, ATT_TQ, ATT_KEYS), F32),
        name="bias_table",
    )(rel_row)


def _softmax_pv(s_tiles, v_tiles):
    def lane_reduce(tiles, combine, reduce):
        if len({t.shape for t in tiles}) == 1:
            tiles = [functools.reduce(combine, tiles)]
        return functools.reduce(combine, [reduce(t, axis=-1, keepdims=True) for t in tiles])

    mx = lane_reduce(s_tiles, jnp.maximum, jnp.max)
    p_tiles = [jnp.exp2(s - mx) for s in s_tiles]
    den = lane_reduce(p_tiles, jnp.add, jnp.sum)
    o = functools.reduce(jnp.add, [jnp.dot(p.astype(BF16), v, preferred_element_type=F32)
                                   for p, v in zip(p_tiles, v_tiles)])
    return o / den


def _attn_prompt_kernel(q_ref, g_ref, kt0_ref, kt1_ref, kt2_ref, v0_ref, v1_ref, v2_ref,
                        bias_ref, o_ref, kp_ref, vp_ref):
    t = pl.program_id(1)
    kt_refs = (kt0_ref, kt1_ref, kt2_ref)
    v_refs = (v0_ref, v1_ref, v2_ref)

    @pl.when(t >= pl.num_programs(1) - KEEP_TILES)
    def _():
        kp_ref[0] = kt2_ref[...].astype(F32).T
        vp_ref[0] = v2_ref[...].astype(F32)

    def body(mask_start):
        lo = lax.broadcasted_iota(jnp.int32, (ATT_TQ, 2 * ATT_HEAD_DIM), 1) < ATT_HEAD_DIM
        for hp in range(ATT_HEADS // 2):
            sl = slice(hp * 128, (hp + 1) * 128)
            qp = q_ref[:, sl]
            zero = jnp.zeros_like(qp)
            q2 = jnp.concatenate([jnp.where(lo, qp, zero), jnp.where(lo, zero, qp)], axis=0)
            s_tiles = []
            for n in range(ATT_TILES):
                ks = slice(n * ATT_TQ, (n + 1) * ATT_TQ)
                s = jnp.dot(q2, kt_refs[n][sl, :], preferred_element_type=F32)
                s = s + jnp.concatenate([bias_ref[2 * hp, :, ks], bias_ref[2 * hp + 1, :, ks]],
                                        axis=0)
                if mask_start and n < ATT_TILES - 1:
                    s = jnp.where(t + n >= ATT_TILES - 1, s, NEG_INF)
                s_tiles.append(s)
            o2 = _softmax_pv(s_tiles, [v_refs[n][:, sl] for n in range(ATT_TILES)])
            o_pair = jnp.where(lo, o2[:ATT_TQ], o2[ATT_TQ:])
            gate = g_ref[:, sl].astype(F32)
            o_ref[:, sl] = (o_pair * _silu(gate)).astype(BF16)

    pl.when(t < ATT_TILES - 1)(functools.partial(body, True))
    pl.when(t >= ATT_TILES - 1)(functools.partial(body, False))


def _attn_prompt(p, kt, bias, nb, seq):
    nt = seq // ATT_TQ

    def tile_row(n):
        return lambda b, t: b * nt + jnp.maximum(t - (ATT_TILES - 1) + n, 0)

    in_specs = [
        pl.BlockSpec((ATT_TQ, D_ATT), lambda b, t: (b * nt + t, COL_Q // D_ATT)),
        pl.BlockSpec((ATT_TQ, D_ATT), lambda b, t: (b * nt + t, COL_G // D_ATT)),
    ]
    in_specs += [pl.BlockSpec((D_ATT, ATT_TQ), lambda b, t, r=tile_row(n): (0, r(b, t)))
                 for n in range(ATT_TILES)]
    in_specs += [pl.BlockSpec((ATT_TQ, D_ATT), lambda b, t, r=tile_row(n): (r(b, t), COL_V // D_ATT))
                 for n in range(ATT_TILES)]
    in_specs += [pl.BlockSpec((ATT_HEADS, ATT_TQ, ATT_KEYS), lambda b, t: (0, 0, 0),
                              pipeline_mode=pl.Buffered(1))]
    keep_spec = pl.BlockSpec((1, ATT_TQ, D_ATT),
                             lambda b, t: (b, jnp.maximum(t - (nt - KEEP_TILES), 0), 0))
    keep_shape = jax.ShapeDtypeStruct((nb, KEEP_TILES * ATT_TQ, D_ATT), F32)
    return pl.pallas_call(
        _attn_prompt_kernel,
        grid=(nb, nt),
        in_specs=in_specs,
        out_specs=[pl.BlockSpec((ATT_TQ, D_ATT), lambda b, t: (b * nt + t, 0)),
                   keep_spec, keep_spec],
        out_shape=[jax.ShapeDtypeStruct((nb * seq, D_ATT), BF16), keep_shape, keep_shape],
        compiler_params=pltpu.CompilerParams(
            dimension_semantics=("arbitrary", "arbitrary"),
            vmem_limit_bytes=VMEM_LIMIT),
        name="attn_prompt",
    )(p, p, kt, kt, kt, p, p, p, bias)


def _attn_step_kernel(q_ref, g_ref, kn_ref, vn_ref, kc_ref, vc_ref, bias_ref,
                      o_ref, ks_ref, vs_ref):
    keep = LEFT_CHUNKS * CHUNK
    ks_ref[0, 0:keep - CHUNK, :] = kc_ref[0, CHUNK:keep, :]
    ks_ref[0, keep - CHUNK:keep, :] = kn_ref[...].astype(F32)
    vs_ref[0, 0:keep - CHUNK, :] = vc_ref[0, CHUNK:keep, :]
    vs_ref[0, keep - CHUNK:keep, :] = vn_ref[...].astype(F32)
    lo = lax.broadcasted_iota(jnp.int32, (CHUNK, 2 * ATT_HEAD_DIM), 1) < ATT_HEAD_DIM

    for hp in range(ATT_HEADS // 2):
        sl = slice(hp * 128, (hp + 1) * 128)
        qp = q_ref[:, sl]
        k_tiles = [kc_ref[0, :, sl].astype(BF16), kn_ref[:, sl]]
        v_tiles = [vc_ref[0, :, sl].astype(BF16), vn_ref[:, sl]]
        zero = jnp.zeros_like(qp)
        q2 = jnp.concatenate([jnp.where(lo, qp, zero), jnp.where(lo, zero, qp)], axis=0)
        s_tiles = []
        for n, (k0, k1) in enumerate(((0, keep), (keep, BAND))):
            s = lax.dot_general(q2, k_tiles[n], (((1,), (1,)), ((), ())),
                                preferred_element_type=F32)
            s_tiles.append(s + jnp.concatenate([bias_ref[2 * hp, 0:CHUNK, k0:k1],
                                                bias_ref[2 * hp + 1, 0:CHUNK, k0:k1]], axis=0))
        o2 = _softmax_pv(s_tiles, v_tiles)
        o_pair = jnp.where(lo, o2[:CHUNK], o2[CHUNK:])
        gate = g_ref[:, sl].astype(F32)
        o_ref[:, sl] = (o_pair * _silu(gate)).astype(BF16)


def _attn_step(p, k_new, cache_k, cache_v, bias, nb):
    keep = LEFT_CHUNKS * CHUNK
    new = lambda col: pl.BlockSpec((CHUNK, D_ATT), lambda b: (b, col))
    cache = pl.BlockSpec((1, keep, D_ATT), lambda b: (b, 0, 0))
    return pl.pallas_call(
        _attn_step_kernel,
        grid=(nb,),
        in_specs=[new(COL_Q // D_ATT), new(COL_G // D_ATT), new(0),
                  new(COL_V // D_ATT), cache, cache,
                  pl.BlockSpec((ATT_HEADS, ATT_TQ, ATT_KEYS), lambda b: (0, 0, 0),
                               pipeline_mode=pl.Buffered(1))],
        out_specs=[pl.BlockSpec((CHUNK, D_ATT), lambda b: (b, 0)), cache, cache],
        out_shape=[jax.ShapeDtypeStruct((nb * CHUNK, D_ATT), BF16),
                   jax.ShapeDtypeStruct((nb, keep, D_ATT), F32),
                   jax.ShapeDtypeStruct((nb, keep, D_ATT), F32)],
        compiler_params=pltpu.CompilerParams(
            dimension_semantics=("arbitrary",),
            vmem_limit_bytes=VMEM_LIMIT),
        name="attn_step",
    )(p, p, k_new, p, cache_k, cache_v, bias)


OUT_TM = 512


def _out_kernel(yz_ref, og_ref, gs_ref, ga_ref, x_ref, wos_ref, woa_ref, wo_ref, fnw_ref, y_ref):
    y_ssm = jnp.dot(yz_ref[...], wos_ref[...], preferred_element_type=F32)
    y_att = jnp.dot(og_ref[...], woa_ref[...], preferred_element_type=F32)
    merged = (_sigmoid(gs_ref[...].astype(F32)) * y_ssm
              + _sigmoid(ga_ref[...].astype(F32)) * y_att)
    h = x_ref[...] + jnp.dot(merged.astype(BF16), wo_ref[...], preferred_element_type=F32)
    ms = jnp.mean(h * h, axis=-1, keepdims=True)
    y_ref[...] = (h * lax.rsqrt(ms + NORM_EPS)) * fnw_ref[...]


def _out(yz, og, p, x2d, w_out_ssm, w_out_att, w_o, final_norm_w):
    n = x2d.shape[0]
    tm = min(OUT_TM, n)
    const = lambda i: (0, 0)
    return pl.pallas_call(
        _out_kernel,
        grid=(n // tm,),
        in_specs=[
            pl.BlockSpec((tm, D_INNER), lambda i: (i, 0)),
            pl.BlockSpec((tm, D_ATT), lambda i: (i, 0)),
            pl.BlockSpec((tm, D_MODEL), lambda i: (i, COL_GS // D_MODEL)),
            pl.BlockSpec((tm, D_MODEL), lambda i: (i, COL_GA // D_MODEL)),
            pl.BlockSpec((tm, D_MODEL), lambda i: (i, 0)),
            pl.BlockSpec((D_INNER, D_MODEL), const),
            pl.BlockSpec((D_ATT, D_MODEL), const),
            pl.BlockSpec((D_MODEL, D_MODEL), const),
            pl.BlockSpec((1, D_MODEL), const),
        ],
        out_specs=pl.BlockSpec((tm, D_MODEL), lambda i: (i, 0)),
        out_shape=jax.ShapeDtypeStruct((n, D_MODEL), F32),
        compiler_params=pltpu.CompilerParams(
            dimension_semantics=("arbitrary",),
            vmem_limit_bytes=VMEM_LIMIT),
        name="out",
    )(yz, og, p, p, x2d, w_out_ssm, w_out_att, w_o, final_norm_w)


def _rel_bias_rows(rel_bias):
    d = ATT_TILES * ATT_TQ - jnp.arange(BIAS_ROW)
    idx = jnp.clip(d, -REL_CLIP, REL_CLIP) + REL_CLIP
    return rel_bias[:, idx].astype(F32).reshape(ATT_HEADS, 1, BIAS_ROW)


def kernel(x_prompt, x_sample, state_ssm, state_conv, cache_k, cache_v, norm_w, w_in, conv_w,
           conv_b, dt_bias, a_log, d_skip, ssm_norm_w, w_out_ssm, rel_bias, w_out_att, w_o,
           final_norm_w):
    bp, lp, _ = x_prompt.shape
    bs, ls, _ = x_sample.shape
    ncp = lp // CHUNK
    assert ls == CHUNK and cache_k.shape[2] == LEFT_CHUNKS * CHUNK

    w = w_in[0]
    o = 0
    parts = {}
    for name, size in (("z", D_INNER), ("xbc", CONV_DIM), ("dt", SSM_HEADS), ("q", D_ATT),
                       ("k", D_ATT), ("v", D_ATT), ("g", D_ATT), ("gs", D_MODEL), ("ga", D_MODEL)):
        parts[name] = w[:, o:o + size]
        o += size
    w_main = jnp.concatenate(
        [parts["z"], parts["xbc"], parts["q"] * (ATT_HEAD_DIM ** -0.5 * LOG2E), parts["v"],
         parts["g"], parts["gs"], parts["ga"]], axis=1).astype(BF16)
    w_kt = parts["k"].T.astype(BF16)
    w_dt = jnp.pad(parts["dt"], ((0, 0), (0, DT_PAD - SSM_HEADS))).astype(BF16)
    nw = norm_w[0].reshape(1, D_MODEL)
    cw = conv_w[0]
    cb = conv_b[0].reshape(1, CONV_DIM)
    dtb = jnp.pad(dt_bias[0], (0, DT_PAD - SSM_HEADS)).reshape(1, DT_PAD)
    alog = jnp.pad(a_log[0], (0, DT_PAD - SSM_HEADS)).reshape(1, DT_PAD)
    dskip_x = jnp.repeat(d_skip[0], SSM_HEAD_DIM).reshape(1, D_INNER)
    snw = ssm_norm_w[0].reshape(1, D_INNER)
    wos = w_out_ssm[0].astype(BF16)
    woa = w_out_att[0].astype(BF16)
    wo = w_o[0].astype(BF16)
    fnw = final_norm_w.reshape(1, D_MODEL)
    bias = _bias_table(_rel_bias_rows(rel_bias[0]))

    ssd_params = (cw, cb, dtb, alog, dskip_x, snw)
    out_params = (wos, woa, wo, fnw)

    xp = x_prompt.reshape(bp * lp, D_MODEL)
    p_p, dt_p, kt_p = _proj(xp, nw, w_main, w_dt, w_kt)
    h0_p = jnp.zeros((bp, D_INNER, D_STATE), F32)
    conv0_p = jnp.zeros((bp, 8, CONV_DIM), F32)
    yz_p, ssm_p, tail_p = _ssd(p_p, dt_p, h0_p, conv0_p, *ssd_params, nb=bp, nc=ncp)
    og_p, k_p, v_p = _attn_prompt(p_p, kt_p, bias, nb=bp, seq=lp)
    y_p = _out(yz_p, og_p, p_p, xp, *out_params).reshape(bp, lp, D_MODEL)
    keep = LEFT_CHUNKS * CHUNK
    k_p = k_p.reshape(1, bp, keep, ATT_HEADS, ATT_HEAD_DIM)
    v_p = v_p.reshape(1, bp, keep, ATT_HEADS, ATT_HEAD_DIM)

    xs = x_sample.reshape(bs * ls, D_MODEL)
    p_s, dt_s, kt_s = _proj(xs, nw, w_main, w_dt, w_kt)
    h0_s = state_ssm[0].reshape(bs, D_INNER, D_STATE)
    conv0_s = jnp.pad(state_conv[0], ((0, 0), (8 - (CONV_W - 1), 0), (0, 0)))
    yz_s, ssm_s, tail_s = _ssd(p_s, dt_s, h0_s, conv0_s, *ssd_params, nb=bs, nc=1)
    og_s, k_s, v_s = _attn_step(p_s, kt_s.T, cache_k[0].reshape(bs, keep, D_ATT),
                                cache_v[0].reshape(bs, keep, D_ATT), bias, nb=bs)
    y_s = _out(yz_s, og_s, p_s, xs, *out_params).reshape(bs, ls, D_MODEL)
    k_s = k_s.reshape(1, bs, keep, ATT_HEADS, ATT_HEAD_DIM)
    v_s = v_s.reshape(1, bs, keep, ATT_HEADS, ATT_HEAD_DIM)

    def states(ssm, tail, nb):
        return (ssm.reshape(1, nb, SSM_HEADS, SSM_HEAD_DIM, D_STATE),
                tail[:, 8 - (CONV_W - 1):, :].reshape(1, nb, CONV_W - 1, CONV_DIM))

    ssm_p5, conv_p4 = states(ssm_p, tail_p, bp)
    ssm_s5, conv_s4 = states(ssm_s, tail_s, bs)
    return (y_p, y_s, ssm_p5, conv_p4, k_p, v_p, ssm_s5, conv_s4, k_s, v_s)
```

```python
import functools

import jax
import jax.numpy as jnp
from jax import lax
from jax.experimental import pallas as pl
from jax.experimental.pallas import tpu as pltpu

D_MODEL = 1024
CHUNK = 64
D_INNER = 2048
SSM_HEAD_DIM = 64
SSM_HEADS = 32
N_GROUPS = 4
D_STATE = 128
GROUP_W = D_INNER // N_GROUPS
BC_W = N_GROUPS * D_STATE
CONV_W = 4
CONV_DIM = D_INNER + 2 * BC_W
ATT_HEADS = 16
ATT_HEAD_DIM = 64
D_ATT = 1024
LEFT_CHUNKS = 8
BAND_CHUNKS = LEFT_CHUNKS + 1
BAND = BAND_CHUNKS * CHUNK
REL_CLIP = 256
NORM_EPS = 1e-5
NEG_INF = -1e30
LOG2E = 1.4426950408889634

COL_Z = 0
COL_X = 2048
COL_B = 4096
COL_C = 4608
COL_Q = 5120
COL_V = 6144
COL_G = 7168
COL_GS = 8192
COL_GA = 9216
P_COLS = 10240
DT_PAD = 128

VMEM_LIMIT = 56 * 1024 * 1024

F32 = jnp.float32
BF16 = jnp.bfloat16


def _sigmoid(x):
    return 0.5 * jnp.tanh(0.5 * x) + 0.5


def _silu(x):
    h = 0.5 * x
    return h * jnp.tanh(h) + h


def _softplus(x):
    return jnp.maximum(x, 0.0) + jnp.log(1.0 + jnp.exp(-jnp.abs(x)))


PROJ_TM = 1024
PROJ_TN = 2560


def _proj_kernel(x_ref, nw_ref, wt_ref, wdt_ref, wkt_ref, p_ref, dt_ref, kt_ref, xn_ref):
    @pl.when(pl.program_id(1) == 0)
    def _():
        x = x_ref[...]
        ms = jnp.mean(x * x, axis=-1, keepdims=True)
        xn = (x * lax.rsqrt(ms + NORM_EPS)) * nw_ref[...]
        xn_ref[...] = xn.astype(BF16)
        dt_ref[...] = jnp.dot(xn_ref[...], wdt_ref[...], preferred_element_type=F32)
        kt_ref[...] = lax.dot_general(wkt_ref[...], xn_ref[...], (((1,), (1,)), ((), ())),
                                      preferred_element_type=F32).astype(BF16)

    p_ref[...] = lax.dot_general(xn_ref[...], wt_ref[...], (((1,), (1,)), ((), ())),
                                 preferred_element_type=F32).astype(BF16)


def _proj(x2d, norm_w, w_main, w_dt, w_kt):
    n = x2d.shape[0]
    tm = min(PROJ_TM, n)
    return pl.pallas_call(
        _proj_kernel,
        grid=(n // tm, P_COLS // PROJ_TN),
        in_specs=[
            pl.BlockSpec((tm, D_MODEL), lambda i, j: (i, 0)),
            pl.BlockSpec((1, D_MODEL), lambda i, j: (0, 0)),
            pl.BlockSpec((PROJ_TN, D_MODEL), lambda i, j: (j, 0)),
            pl.BlockSpec((D_MODEL, DT_PAD), lambda i, j: (0, 0)),
            pl.BlockSpec((D_ATT, D_MODEL), lambda i, j: (0, 0)),
        ],
        out_specs=[
            pl.BlockSpec((tm, PROJ_TN), lambda i, j: (i, j)),
            pl.BlockSpec((tm, DT_PAD), lambda i, j: (i, 0)),
            pl.BlockSpec((D_ATT, tm), lambda i, j: (0, i)),
        ],
        out_shape=[
            jax.ShapeDtypeStruct((n, P_COLS), BF16),
            jax.ShapeDtypeStruct((n, DT_PAD), F32),
            jax.ShapeDtypeStruct((D_ATT, n), BF16),
        ],
        scratch_shapes=[pltpu.VMEM((tm, D_MODEL), BF16)],
        compiler_params=pltpu.CompilerParams(
            dimension_semantics=("arbitrary", "arbitrary"),
            vmem_limit_bytes=VMEM_LIMIT),
        name="proj",
    )(x2d, norm_w, w_main, w_dt, w_kt)


SSD_SUB = 4


def _shift_matrix():
    t = lax.broadcasted_iota(jnp.int32, (CHUNK, (CONV_W - 1) * CHUNK), 0)
    c = lax.broadcasted_iota(jnp.int32, (CHUNK, (CONV_W - 1) * CHUNK), 1)
    return (c % CHUNK == t - (c // CHUNK + 1)).astype(BF16)


def _conv_silu(u_bf16, tail_ref, cols, w, b, shift_mat):
    u = u_bf16.astype(F32)
    taps = [w[CONV_W - 1 - s:CONV_W - s, :] for s in range(CONV_W)]
    delayed = jnp.concatenate([(taps[s] * u).astype(BF16) for s in range(1, CONV_W)], axis=0)
    acc = b + taps[0] * u + jnp.dot(shift_mat, delayed, preferred_element_type=F32)
    tail = tail_ref[:, cols]
    row8 = lax.broadcasted_iota(jnp.int32, (8, 1), 0)
    head = jnp.zeros_like(tail)
    for s in range(1, CONV_W):
        head = head + jnp.where(row8 < s, taps[s] * pltpu.roll(tail, s, axis=0), 0.0)
    acc = jnp.concatenate([acc[0:8, :] + head, acc[8:, :]], axis=0)
    tail_ref[:, cols] = u[CHUNK - 8:, :]
    return _silu(acc)


def _ssd_kernel(z_ref, x_ref, b_ref, c_ref, dtr_ref, h0_ref, conv0_ref,
                cw_ref, cb_ref, dtb_ref, alog_ref, dskip_ref, nw_ref,
                yz_ref, hout_ref, tailout_ref,
                ht_ref, tail_ref, xw_ref, y_ref, ea_ref, *, nsub):
    c = pl.program_id(1)
    n_chunks = pl.num_programs(1)

    @pl.when(c == 0)
    def _():
        for g in range(N_GROUPS):
            sl = slice(g * GROUP_W, (g + 1) * GROUP_W)
            ht_ref[:, sl] = h0_ref[0, sl, :].T
        tail_ref[...] = conv0_ref[0]

    def chunk(rows):
        cx = slice(0, D_INNER)
        cbm = slice(D_INNER, D_INNER + BC_W)
        ccm = slice(D_INNER + BC_W, CONV_DIM)
        shift_mat = _shift_matrix()
        xs = _conv_silu(x_ref[rows, :], tail_ref, cx, cw_ref[:, cx], cb_ref[:, cx], shift_mat)
        bm = _conv_silu(b_ref[rows, :], tail_ref, cbm, cw_ref[:, cbm], cb_ref[:, cbm], shift_mat)
        cm = _conv_silu(c_ref[rows, :], tail_ref, ccm, cw_ref[:, ccm], cb_ref[:, ccm], shift_mat)
        bm_b = bm.astype(BF16)
        cm_b = cm.astype(BF16)

        dt = _softplus(dtr_ref[rows, :] + dtb_ref[...])
        adt = dt * (-jnp.exp(alog_ref[...]) * LOG2E)
        li = lax.broadcasted_iota(jnp.int32, (CHUNK, CHUNK), 0)
        si = lax.broadcasted_iota(jnp.int32, (CHUNK, CHUNK), 1)
        tri = (li >= si).astype(F32)
        acs = jnp.dot(tri, adt, precision=lax.Precision.HIGHEST,
                      preferred_element_type=F32)
        s2 = lax.broadcasted_iota(jnp.int32, (CHUNK, 2 * CHUNK), 0)
        l2 = lax.broadcasted_iota(jnp.int32, (CHUNK, 2 * CHUNK), 1) % CHUNK
        tri_t2 = (l2 >= s2).astype(F32)
        acs_t2 = lax.dot_general(adt, tri_t2, (((0,), (0,)), ((), ())),
                                 precision=lax.Precision.HIGHEST,
                                 preferred_element_type=F32)

        lane = lax.broadcasted_iota(jnp.int32, (CHUNK, 2 * CHUNK), 1)
        lo = lane < CHUNK
        lo_row = lo[0:1, :]
        tri2 = (lax.broadcasted_iota(jnp.int32, (CHUNK, 2 * CHUNK), 0) >= (lane % CHUNK))

        yoff = []
        cb2 = []
        for g in range(N_GROUPS):
            sl = slice(g * GROUP_W, (g + 1) * GROUP_W)
            ns = slice(g * D_STATE, (g + 1) * D_STATE)
            yoff.append(jnp.dot(cm_b[:, ns], ht_ref[:, sl].astype(BF16),
                                preferred_element_type=F32))
            b2 = jnp.concatenate([bm_b[:, ns], bm_b[:, ns]], axis=0)
            cb2.append(lax.dot_general(cm_b[:, ns], b2, (((1,), (1,)), ((), ())),
                                       preferred_element_type=F32))

        for j in range(SSM_HEADS // 2):
            g = j // (SSM_HEADS // 2 // N_GROUPS)
            r0, r1 = 2 * j, 2 * j + 1
            sl = slice(j * 128, (j + 1) * 128)
            gsl = slice((j % 4) * 128, (j % 4 + 1) * 128)
            u_col = jnp.where(lo, acs[:, r0:r0 + 1], acs[:, r1:r1 + 1])
            v_row = jnp.where(lo_row, acs_t2[r0:r0 + 1, :], acs_t2[r1:r1 + 1, :])
            dtp = jnp.where(lo, dt[:, r0:r0 + 1], dt[:, r1:r1 + 1])
            decay = jnp.exp2(jnp.where(tri2, u_col - v_row, -jnp.inf))
            m = (cb2[g] * decay).astype(BF16)
            xs_p = xs[:, sl]
            xdt = xs_p * dtp
            xdt_b = xdt.astype(BF16)
            zero = jnp.zeros_like(xdt_b)
            rhs = jnp.concatenate([jnp.where(lo, xdt_b, zero), jnp.where(lo, zero, xdt_b)], axis=0)
            yd = jnp.dot(m, rhs, preferred_element_type=F32)
            e_u = jnp.exp2(u_col)
            y = yd + yoff[g][:, gsl] * e_u + dskip_ref[:, sl] * xs_p
            to_end = jnp.exp2(u_col[CHUNK - 1:CHUNK, :] - u_col)
            xw_ref[:, sl] = (xdt * to_end).astype(BF16)
            ea_ref[:, sl] = e_u[CHUNK - 8:CHUNK, :]
            zf = z_ref[rows, sl].astype(F32)
            y_ref[:, sl] = y * _silu(zf)

        for g in range(N_GROUPS):
            sl = slice(g * GROUP_W, (g + 1) * GROUP_W)
            ns = slice(g * D_STATE, (g + 1) * D_STATE)
            yg = y_ref[:, sl]
            ms = jnp.mean(yg * yg, axis=-1, keepdims=True)
            yz_ref[rows, sl] = ((yg * lax.rsqrt(ms + NORM_EPS)) * nw_ref[:, sl]).astype(BF16)
            upd = lax.dot_general(bm_b[:, ns], xw_ref[:, sl], (((0,), (0,)), ((), ())),
                                  preferred_element_type=F32)
            ht_ref[:, sl] = ht_ref[:, sl] * ea_ref[7:8, sl] + upd

    if nsub == 1:
        chunk(slice(None))
    else:
        def sub(i, carry):
            chunk(pl.ds(pl.multiple_of(i * CHUNK, CHUNK), CHUNK))
            return carry
        lax.fori_loop(0, nsub, sub, 0)

    @pl.when(c == n_chunks - 1)
    def _():
        for g in range(N_GROUPS):
            sl = slice(g * GROUP_W, (g + 1) * GROUP_W)
            hout_ref[0, sl, :] = ht_ref[:, sl].T
        tailout_ref[0] = tail_ref[...]


def _ssd(p, dtraw, h0, conv0, conv_w, conv_b, dt_bias, a_log, d_skip_x, ssm_norm_w, nb, nc):
    n = nb * nc * CHUNK
    nsub = SSD_SUB if nc % SSD_SUB == 0 else 1
    ns = nc // nsub
    tr = nsub * CHUNK
    row = lambda b, c: b * ns + c
    const = lambda b, c: (0, 0)
    return pl.pallas_call(
        functools.partial(_ssd_kernel, nsub=nsub),
        grid=(nb, ns),
        in_specs=[
            pl.BlockSpec((tr, D_INNER), lambda b, c: (row(b, c), COL_Z // D_INNER)),
            pl.BlockSpec((tr, D_INNER), lambda b, c: (row(b, c), COL_X // D_INNER)),
            pl.BlockSpec((tr, BC_W), lambda b, c: (row(b, c), COL_B // BC_W)),
            pl.BlockSpec((tr, BC_W), lambda b, c: (row(b, c), COL_C // BC_W)),
            pl.BlockSpec((tr, DT_PAD), lambda b, c: (row(b, c), 0)),
            pl.BlockSpec((1, D_INNER, D_STATE), lambda b, c: (b, 0, 0)),
            pl.BlockSpec((1, 8, CONV_DIM), lambda b, c: (b, 0, 0)),
            pl.BlockSpec((CONV_W, CONV_DIM), const),
            pl.BlockSpec((1, CONV_DIM), const),
            pl.BlockSpec((1, DT_PAD), const),
            pl.BlockSpec((1, DT_PAD), const),
            pl.BlockSpec((1, D_INNER), const),
            pl.BlockSpec((1, D_INNER), const),
        ],
        out_specs=[
            pl.BlockSpec((tr, D_INNER), lambda b, c: (row(b, c), 0)),
            pl.BlockSpec((1, D_INNER, D_STATE), lambda b, c: (b, 0, 0)),
            pl.BlockSpec((1, 8, CONV_DIM), lambda b, c: (b, 0, 0)),
        ],
        out_shape=[
            jax.ShapeDtypeStruct((n, D_INNER), BF16),
            jax.ShapeDtypeStruct((nb, D_INNER, D_STATE), F32),
            jax.ShapeDtypeStruct((nb, 8, CONV_DIM), F32),
        ],
        scratch_shapes=[
            pltpu.VMEM((D_STATE, D_INNER), F32),
            pltpu.VMEM((8, CONV_DIM), F32),
            pltpu.VMEM((CHUNK, D_INNER), BF16),
            pltpu.VMEM((CHUNK, D_INNER), F32),
            pltpu.VMEM((8, D_INNER), F32),
        ],
        compiler_params=pltpu.CompilerParams(
            dimension_semantics=("arbitrary", "arbitrary"),
            vmem_limit_bytes=VMEM_LIMIT),
        name="ssd",
    )(p, p, p, p, dtraw, h0, conv0, conv_w, conv_b, dt_bias, a_log, d_skip_x, ssm_norm_w)


ATT_TQ = 256
ATT_TILES = 3
ATT_KEYS = ATT_TILES * ATT_TQ
BIAS_ROW = 1024
KEEP_TILES = LEFT_CHUNKS * CHUNK // ATT_TQ


def _bias_kernel(r_ref, o_ref):
    x = jnp.broadcast_to(r_ref[0], (ATT_TQ, BIAS_ROW))
    tab = pltpu.roll(x, BIAS_ROW - ATT_TQ, axis=1, stride=1, stride_axis=0)[:, :ATT_KEYS]
    qc = lax.broadcasted_iota(jnp.int32, (ATT_TQ, ATT_KEYS), 0) // CHUNK
    kc = lax.broadcasted_iota(jnp.int32, (ATT_TQ, ATT_KEYS), 1) // CHUNK
    ok = (kc >= qc) & (kc <= qc + LEFT_CHUNKS)
    o_ref[0] = jnp.where(ok, tab * LOG2E, NEG_INF)


def _bias_table(rel_row):
    return pl.pallas_call(
        _bias_kernel,
        grid=(ATT_HEADS,),
        in_specs=[pl.BlockSpec((1, 1, BIAS_ROW), lambda h: (h, 0, 0))],
        out_specs=pl.BlockSpec((1, ATT_TQ, ATT_KEYS), lambda h: (h, 0, 0)),
        out_shape=jax.ShapeDtypeStruct((ATT_HEADS, ATT_TQ, ATT_KEYS), F32),
        name="bias_table",
    )(rel_row)


def _softmax_pv(s_tiles, v_tiles):
    def lane_reduce(tiles, combine, reduce):
        if len({t.shape for t in tiles}) == 1:
            tiles = [functools.reduce(combine, tiles)]
        return functools.reduce(combine, [reduce(t, axis=-1, keepdims=True) for t in tiles])

    mx = lane_reduce(s_tiles, jnp.maximum, jnp.max)
    p_tiles = [jnp.exp2(s - mx) for s in s_tiles]
    den = lane_reduce(p_tiles, jnp.add, jnp.sum)
    o = functools.reduce(jnp.add, [jnp.dot(p.astype(BF16), v, preferred_element_type=F32)
                                   for p, v in zip(p_tiles, v_tiles)])
    return o / den


def _attn_prompt_kernel(q_ref, g_ref, kt0_ref, kt1_ref, kt2_ref, v0_ref, v1_ref, v2_ref,
                        bias_ref, o_ref, kp_ref, vp_ref):
    t = pl.program_id(1)
    kt_refs = (kt0_ref, kt1_ref, kt2_ref)
    v_refs = (v0_ref, v1_ref, v2_ref)

    @pl.when(t >= pl.num_programs(1) - KEEP_TILES)
    def _():
        kp_ref[0] = kt2_ref[...].astype(F32).T
        vp_ref[0] = v2_ref[...].astype(F32)

    def body(mask_start):
        lo = lax.broadcasted_iota(jnp.int32, (ATT_TQ, 2 * ATT_HEAD_DIM), 1) < ATT_HEAD_DIM
        for hp in range(ATT_HEADS // 2):
            sl = slice(hp * 128, (hp + 1) * 128)
            qp = q_ref[:, sl]
            zero = jnp.zeros_like(qp)
            q2 = jnp.concatenate([jnp.where(lo, qp, zero), jnp.where(lo, zero, qp)], axis=0)
            s_tiles = []
            for n in range(ATT_TILES):
                ks = slice(n * ATT_TQ, (n + 1) * ATT_TQ)
                s = jnp.dot(q2, kt_refs[n][sl, :], preferred_element_type=F32)
                s = s + jnp.concatenate([bias_ref[2 * hp, :, ks], bias_ref[2 * hp + 1, :, ks]],
                                        axis=0)
                if mask_start and n < ATT_TILES - 1:
                    s = jnp.where(t + n >= ATT_TILES - 1, s, NEG_INF)
                s_tiles.append(s)
            o2 = _softmax_pv(s_tiles, [v_refs[n][:, sl] for n in range(ATT_TILES)])
            o_pair = jnp.where(lo, o2[:ATT_TQ], o2[ATT_TQ:])
            gate = g_ref[:, sl].astype(F32)
            o_ref[:, sl] = (o_pair * _silu(gate)).astype(BF16)

    pl.when(t < ATT_TILES - 1)(functools.partial(body, True))
    pl.when(t >= ATT_TILES - 1)(functools.partial(body, False))


def _attn_prompt(p, kt, bias, nb, seq):
    nt = seq // ATT_TQ

    def tile_row(n):
        return lambda b, t: b * nt + jnp.maximum(t - (ATT_TILES - 1) + n, 0)

    in_specs = [
        pl.BlockSpec((ATT_TQ, D_ATT), lambda b, t: (b * nt + t, COL_Q // D_ATT)),
        pl.BlockSpec((ATT_TQ, D_ATT), lambda b, t: (b * nt + t, COL_G // D_ATT)),
    ]
    in_specs += [pl.BlockSpec((D_ATT, ATT_TQ), lambda b, t, r=tile_row(n): (0, r(b, t)))
                 for n in range(ATT_TILES)]
    in_specs += [pl.BlockSpec((ATT_TQ, D_ATT), lambda b, t, r=tile_row(n): (r(b, t), COL_V // D_ATT))
                 for n in range(ATT_TILES)]
    in_specs += [pl.BlockSpec((ATT_HEADS, ATT_TQ, ATT_KEYS), lambda b, t: (0, 0, 0),
                              pipeline_mode=pl.Buffered(1))]
    keep_spec = pl.BlockSpec((1, ATT_TQ, D_ATT),
                             lambda b, t: (b, jnp.maximum(t - (nt - KEEP_TILES), 0), 0))
    keep_shape = jax.ShapeDtypeStruct((nb, KEEP_TILES * ATT_TQ, D_ATT), F32)
    return pl.pallas_call(
        _attn_prompt_kernel,
        grid=(nb, nt),
        in_specs=in_specs,
        out_specs=[pl.BlockSpec((ATT_TQ, D_ATT), lambda b, t: (b * nt + t, 0)),
                   keep_spec, keep_spec],
        out_shape=[jax.ShapeDtypeStruct((nb * seq, D_ATT), BF16), keep_shape, keep_shape],
        compiler_params=pltpu.CompilerParams(
            dimension_semantics=("arbitrary", "arbitrary"),
            vmem_limit_bytes=VMEM_LIMIT),
        name="attn_prompt",
    )(p, p, kt, kt, kt, p, p, p, bias)


def _attn_step_kernel(q_ref, g_ref, kn_ref, vn_ref, kc_ref, vc_ref, bias_ref,
                      o_ref, ks_ref, vs_ref):
    keep = LEFT_CHUNKS * CHUNK
    ks_ref[0, 0:keep - CHUNK, :] = kc_ref[0, CHUNK:keep, :]
    ks_ref[0, keep - CHUNK:keep, :] = kn_ref[...].astype(F32)
    vs_ref[0, 0:keep - CHUNK, :] = vc_ref[0, CHUNK:keep, :]
    vs_ref[0, keep - CHUNK:keep, :] = vn_ref[...].astype(F32)
    lo = lax.broadcasted_iota(jnp.int32, (CHUNK, 2 * ATT_HEAD_DIM), 1) < ATT_HEAD_DIM

    for hp in range(ATT_HEADS // 2):
        sl = slice(hp * 128, (hp + 1) * 128)
        qp = q_ref[:, sl]
        k_tiles = [kc_ref[0, :, sl].astype(BF16), kn_ref[:, sl]]
        v_tiles = [vc_ref[0, :, sl].astype(BF16), vn_ref[:, sl]]
        zero = jnp.zeros_like(qp)
        q2 = jnp.concatenate([jnp.where(lo, qp, zero), jnp.where(lo, zero, qp)], axis=0)
        s_tiles = []
        for n, (k0, k1) in enumerate(((0, keep), (keep, BAND))):
            s = lax.dot_general(q2, k_tiles[n], (((1,), (1,)), ((), ())),
                                preferred_element_type=F32)
            s_tiles.append(s + jnp.concatenate([bias_ref[2 * hp, 0:CHUNK, k0:k1],
                                                bias_ref[2 * hp + 1, 0:CHUNK, k0:k1]], axis=0))
        o2 = _softmax_pv(s_tiles, v_tiles)
        o_pair = jnp.where(lo, o2[:CHUNK], o2[CHUNK:])
        gate = g_ref[:, sl].astype(F32)
        o_ref[:, sl] = (o_pair * _silu(gate)).astype(BF16)


def _attn_step(p, k_new, cache_k, cache_v, bias, nb):
    keep = LEFT_CHUNKS * CHUNK
    new = lambda col: pl.BlockSpec((CHUNK, D_ATT), lambda b: (b, col))
    cache = pl.BlockSpec((1, keep, D_ATT), lambda b: (b, 0, 0))
    cache_shape = jax.ShapeDtypeStruct((nb, keep, D_ATT), F32)
    return pl.pallas_call(
        _attn_step_kernel,
        grid=(nb,),
        in_specs=[new(COL_Q // D_ATT), new(COL_G // D_ATT), new(0),
                  new(COL_V // D_ATT), cache, cache,
                  pl.BlockSpec((ATT_HEADS, ATT_TQ, ATT_KEYS), lambda b: (0, 0, 0),
                               pipeline_mode=pl.Buffered(1))],
        out_specs=[pl.BlockSpec((CHUNK, D_ATT), lambda b: (b, 0)), cache, cache],
        out_shape=[jax.ShapeDtypeStruct((nb * CHUNK, D_ATT), BF16), cache_shape, cache_shape],
        compiler_params=pltpu.CompilerParams(
            dimension_semantics=("arbitrary",),
            vmem_limit_bytes=VMEM_LIMIT),
        name="attn_step",
    )(p, p, k_new, p, cache_k, cache_v, bias)


OUT_TM = 512


def _out_kernel(yz_ref, og_ref, gs_ref, ga_ref, x_ref, wos_ref, woa_ref, wo_ref, fnw_ref, y_ref):
    y_ssm = jnp.dot(yz_ref[...], wos_ref[...], preferred_element_type=F32)
    y_att = jnp.dot(og_ref[...], woa_ref[...], preferred_element_type=F32)
    merged = (_sigmoid(gs_ref[...].astype(F32)) * y_ssm
              + _sigmoid(ga_ref[...].astype(F32)) * y_att)
    h = x_ref[...] + jnp.dot(merged.astype(BF16), wo_ref[...], preferred_element_type=F32)
    ms = jnp.mean(h * h, axis=-1, keepdims=True)
    y_ref[...] = (h * lax.rsqrt(ms + NORM_EPS)) * fnw_ref[...]


def _out(yz, og, p, x2d, w_out_ssm, w_out_att, w_o, final_norm_w):
    n = x2d.shape[0]
    tm = min(OUT_TM, n)
    const = lambda i: (0, 0)
    return pl.pallas_call(
        _out_kernel,
        grid=(n // tm,),
        in_specs=[
            pl.BlockSpec((tm, D_INNER), lambda i: (i, 0)),
            pl.BlockSpec((tm, D_ATT), lambda i: (i, 0)),
            pl.BlockSpec((tm, D_MODEL), lambda i: (i, COL_GS // D_MODEL)),
            pl.BlockSpec((tm, D_MODEL), lambda i: (i, COL_GA // D_MODEL)),
            pl.BlockSpec((tm, D_MODEL), lambda i: (i, 0)),
            pl.BlockSpec((D_INNER, D_MODEL), const),
            pl.BlockSpec((D_ATT, D_MODEL), const),
            pl.BlockSpec((D_MODEL, D_MODEL), const),
            pl.BlockSpec((1, D_MODEL), const),
        ],
        out_specs=pl.BlockSpec((tm, D_MODEL), lambda i: (i, 0)),
        out_shape=jax.ShapeDtypeStruct((n, D_MODEL), F32),
        compiler_params=pltpu.CompilerParams(
            dimension_semantics=("arbitrary",),
            vmem_limit_bytes=VMEM_LIMIT),
        name="out",
    )(yz, og, p, p, x2d, w_out_ssm, w_out_att, w_o, final_norm_w)


def _rel_bias_rows(rel_bias):
    d = ATT_TILES * ATT_TQ - jnp.arange(BIAS_ROW)
    idx = jnp.clip(d, -REL_CLIP, REL_CLIP) + REL_CLIP
    return rel_bias[:, idx].astype(F32).reshape(ATT_HEADS, 1, BIAS_ROW)


def kernel(x_prompt, x_sample, state_ssm, state_conv, cache_k, cache_v, norm_w, w_in, conv_w,
           conv_b, dt_bias, a_log, d_skip, ssm_norm_w, w_out_ssm, rel_bias, w_out_att, w_o,
           final_norm_w):
    bp, lp, _ = x_prompt.shape
    bs, ls, _ = x_sample.shape
    ncp = lp // CHUNK
    assert ls == CHUNK and cache_k.shape[2] == LEFT_CHUNKS * CHUNK

    wt = w_in[0].T
    o = 0
    parts = {}
    for name, size in (("z", D_INNER), ("xbc", CONV_DIM), ("dt", SSM_HEADS), ("q", D_ATT),
                       ("k", D_ATT), ("v", D_ATT), ("g", D_ATT), ("gs", D_MODEL), ("ga", D_MODEL)):
        parts[name] = wt[o:o + size]
        o += size
    w_main = jnp.concatenate(
        [parts["z"], parts["xbc"], parts["q"] * (ATT_HEAD_DIM ** -0.5 * LOG2E), parts["v"],
         parts["g"], parts["gs"], parts["ga"]], axis=0).astype(BF16)
    w_kt = parts["k"].astype(BF16)
    w_dt = jnp.pad(parts["dt"].T, ((0, 0), (0, DT_PAD - SSM_HEADS))).astype(BF16)
    nw = norm_w[0].reshape(1, D_MODEL)
    cw = conv_w[0]
    cb = conv_b[0].reshape(1, CONV_DIM)
    dtb = jnp.pad(dt_bias[0], (0, DT_PAD - SSM_HEADS)).reshape(1, DT_PAD)
    alog = jnp.pad(a_log[0], (0, DT_PAD - SSM_HEADS)).reshape(1, DT_PAD)
    dskip_x = jnp.repeat(d_skip[0], SSM_HEAD_DIM).reshape(1, D_INNER)
    snw = ssm_norm_w[0].reshape(1, D_INNER)
    wos = w_out_ssm[0].astype(BF16)
    woa = w_out_att[0].astype(BF16)
    wo = w_o[0].astype(BF16)
    fnw = final_norm_w.reshape(1, D_MODEL)
    bias = _bias_table(_rel_bias_rows(rel_bias[0]))

    ssd_params = (cw, cb, dtb, alog, dskip_x, snw)
    out_params = (wos, woa, wo, fnw)

    xp = x_prompt.reshape(bp * lp, D_MODEL)
    p_p, dt_p, kt_p = _proj(xp, nw, w_main, w_dt, w_kt)
    h0_p = jnp.zeros((bp, D_INNER, D_STATE), F32)
    conv0_p = jnp.zeros((bp, 8, CONV_DIM), F32)
    yz_p, ssm_p, tail_p = _ssd(p_p, dt_p, h0_p, conv0_p, *ssd_params, nb=bp, nc=ncp)
    og_p, k_p, v_p = _attn_prompt(p_p, kt_p, bias, nb=bp, seq=lp)
    y_p = _out(yz_p, og_p, p_p, xp, *out_params).reshape(bp, lp, D_MODEL)
    keep = LEFT_CHUNKS * CHUNK
    k_p = k_p.reshape(1, bp, keep, ATT_HEADS, ATT_HEAD_DIM)
    v_p = v_p.reshape(1, bp, keep, ATT_HEADS, ATT_HEAD_DIM)

    xs = x_sample.reshape(bs * ls, D_MODEL)
    p_s, dt_s, kt_s = _proj(xs, nw, w_main, w_dt, w_kt)
    h0_s = state_ssm[0].reshape(bs, D_INNER, D_STATE)
    conv0_s = jnp.pad(state_conv[0], ((0, 0), (8 - (CONV_W - 1), 0), (0, 0)))
    yz_s, ssm_s, tail_s = _ssd(p_s, dt_s, h0_s, conv0_s, *ssd_params, nb=bs, nc=1)
    og_s, k_s, v_s = _attn_step(p_s, kt_s.T, cache_k[0].reshape(bs, keep, D_ATT),
                                cache_v[0].reshape(bs, keep, D_ATT), bias, nb=bs)
    y_s = _out(yz_s, og_s, p_s, xs, *out_params).reshape(bs, ls, D_MODEL)
    k_s = k_s.reshape(1, bs, keep, ATT_HEADS, ATT_HEAD_DIM)
    v_s = v_s.reshape(1, bs, keep, ATT_HEADS, ATT_HEAD_DIM)

    def states(ssm, tail, nb):
        return (ssm.reshape(1, nb, SSM_HEADS, SSM_HEAD_DIM, D_STATE),
                tail[:, 8 - (CONV_W - 1):, :].reshape(1, nb, CONV_W - 1, CONV_DIM))

    ssm_p5, conv_p4 = states(ssm_p, tail_p, bp)
    ssm_s5, conv_s4 = states(ssm_s, tail_s, bs)
    return (y_p, y_s, ssm_p5, conv_p4, k_p, v_p, ssm_s5, conv_s4, k_s, v_s)
```

```python
import functools

import jax
import jax.numpy as jnp
from jax import lax
from jax.experimental import pallas as pl
from jax.experimental.pallas import tpu as pltpu

D_MODEL = 1024
CHUNK = 64
D_INNER = 2048
SSM_HEAD_DIM = 64
SSM_HEADS = 32
N_GROUPS = 4
D_STATE = 128
GROUP_W = D_INNER // N_GROUPS
BC_W = N_GROUPS * D_STATE
CONV_W = 4
CONV_DIM = D_INNER + 2 * BC_W
ATT_HEADS = 16
ATT_HEAD_DIM = 64
D_ATT = 1024
LEFT_CHUNKS = 8
BAND_CHUNKS = LEFT_CHUNKS + 1
BAND = BAND_CHUNKS * CHUNK
REL_CLIP = 256
NORM_EPS = 1e-5
NEG_INF = -1e30
LOG2E = 1.4426950408889634

COL_Z = 0
COL_X = 2048
COL_B = 4096
COL_C = 4608
COL_Q = 5120
COL_V = 6144
COL_G = 7168
COL_GS = 8192
COL_GA = 9216
P_COLS = 10240
DT_PAD = 128

VMEM_LIMIT = 56 * 1024 * 1024

F32 = jnp.float32
BF16 = jnp.bfloat16


def _sigmoid(x):
    return 0.5 * jnp.tanh(0.5 * x) + 0.5


def _silu(x):
    h = 0.5 * x
    return h * jnp.tanh(h) + h


def _softplus(x):
    return jnp.maximum(x, 0.0) + jnp.log(1.0 + jnp.exp(-jnp.abs(x)))


PROJ_TM = 1024
PROJ_TN = 2560


def _proj_kernel(x_ref, nw_ref, wt_ref, wdt_ref, wkt_ref, p_ref, dt_ref, kt_ref, xn_ref):
    @pl.when(pl.program_id(1) == 0)
    def _():
        x = x_ref[...]
        ms = jnp.mean(x * x, axis=-1, keepdims=True)
        xn = (x * lax.rsqrt(ms + NORM_EPS)) * nw_ref[...]
        xn_ref[...] = xn.astype(BF16)
        dt_ref[...] = jnp.dot(xn_ref[...], wdt_ref[...], preferred_element_type=F32)
        kt_ref[...] = lax.dot_general(wkt_ref[...], xn_ref[...], (((1,), (1,)), ((), ())),
                                      preferred_element_type=F32).astype(BF16)

    p_ref[...] = lax.dot_general(xn_ref[...], wt_ref[...], (((1,), (1,)), ((), ())),
                                 preferred_element_type=F32).astype(BF16)


def _proj(x2d, norm_w, w_main, w_dt, w_kt):
    n = x2d.shape[0]
    tm = min(PROJ_TM, n)
    return pl.pallas_call(
        _proj_kernel,
        grid=(n // tm, P_COLS // PROJ_TN),
        in_specs=[
            pl.BlockSpec((tm, D_MODEL), lambda i, j: (i, 0)),
            pl.BlockSpec((1, D_MODEL), lambda i, j: (0, 0)),
            pl.BlockSpec((PROJ_TN, D_MODEL), lambda i, j: (j, 0)),
            pl.BlockSpec((D_MODEL, DT_PAD), lambda i, j: (0, 0)),
            pl.BlockSpec((D_ATT, D_MODEL), lambda i, j: (0, 0)),
        ],
        out_specs=[
            pl.BlockSpec((tm, PROJ_TN), lambda i, j: (i, j)),
            pl.BlockSpec((tm, DT_PAD), lambda i, j: (i, 0)),
            pl.BlockSpec((D_ATT, tm), lambda i, j: (0, i)),
        ],
        out_shape=[
            jax.ShapeDtypeStruct((n, P_COLS), BF16),
            jax.ShapeDtypeStruct((n, DT_PAD), F32),
            jax.ShapeDtypeStruct((D_ATT, n), BF16),
        ],
        scratch_shapes=[pltpu.VMEM((tm, D_MODEL), BF16)],
        compiler_params=pltpu.CompilerParams(
            dimension_semantics=("arbitrary", "arbitrary"),
            vmem_limit_bytes=VMEM_LIMIT),
        name="proj",
    )(x2d, norm_w, w_main, w_dt, w_kt)


SSD_SUB = 4


def _shift_matrix():
    t = lax.broadcasted_iota(jnp.int32, (CHUNK, (CONV_W - 1) * CHUNK), 0)
    c = lax.broadcasted_iota(jnp.int32, (CHUNK, (CONV_W - 1) * CHUNK), 1)
    return (c % CHUNK == t - (c // CHUNK + 1)).astype(BF16)


def _conv_silu(u_bf16, tail_ref, cols, w, b, shift_mat):
    u = u_bf16.astype(F32)
    taps = [w[CONV_W - 1 - s:CONV_W - s, :] for s in range(CONV_W)]
    delayed = jnp.concatenate([(taps[s] * u).astype(BF16) for s in range(1, CONV_W)], axis=0)
    acc = b + taps[0] * u + jnp.dot(shift_mat, delayed, preferred_element_type=F32)
    tail = tail_ref[:, cols]
    row8 = lax.broadcasted_iota(jnp.int32, (8, 1), 0)
    head = jnp.zeros_like(tail)
    for s in range(1, CONV_W):
        head = head + jnp.where(row8 < s, taps[s] * pltpu.roll(tail, s, axis=0), 0.0)
    acc = jnp.concatenate([acc[0:8, :] + head, acc[8:, :]], axis=0)
    tail_ref[:, cols] = u[CHUNK - 8:, :]
    return _silu(acc)


def _ssd_kernel(z_ref, x_ref, b_ref, c_ref, dtr_ref, h0_ref, conv0_ref,
                cw_ref, cb_ref, dtb_ref, alog_ref, dskip_ref, nw_ref,
                yz_ref, hout_ref, tailout_ref,
                ht_ref, tail_ref, xw_ref, y_ref, ea_ref, *, nsub):
    c = pl.program_id(1)
    n_chunks = pl.num_programs(1)

    @pl.when(c == 0)
    def _():
        for g in range(N_GROUPS):
            sl = slice(g * GROUP_W, (g + 1) * GROUP_W)
            ht_ref[:, sl] = h0_ref[0, sl, :].T
        tail_ref[...] = conv0_ref[0]

    def chunk(rows):
        cx = slice(0, D_INNER)
        cbm = slice(D_INNER, D_INNER + BC_W)
        ccm = slice(D_INNER + BC_W, CONV_DIM)
        shift_mat = _shift_matrix()
        xs = _conv_silu(x_ref[rows, :], tail_ref, cx, cw_ref[:, cx], cb_ref[:, cx], shift_mat)
        bm = _conv_silu(b_ref[rows, :], tail_ref, cbm, cw_ref[:, cbm], cb_ref[:, cbm], shift_mat)
        cm = _conv_silu(c_ref[rows, :], tail_ref, ccm, cw_ref[:, ccm], cb_ref[:, ccm], shift_mat)
        bm_b = bm.astype(BF16)
        cm_b = cm.astype(BF16)

        dt = _softplus(dtr_ref[rows, :] + dtb_ref[...])
        adt = dt * (-jnp.exp(alog_ref[...]) * LOG2E)
        li = lax.broadcasted_iota(jnp.int32, (CHUNK, CHUNK), 0)
        si = lax.broadcasted_iota(jnp.int32, (CHUNK, CHUNK), 1)
        tri = (li >= si).astype(F32)
        acs = jnp.dot(tri, adt, precision=lax.Precision.HIGHEST,
                      preferred_element_type=F32)
        s2 = lax.broadcasted_iota(jnp.int32, (CHUNK, 2 * CHUNK), 0)
        l2 = lax.broadcasted_iota(jnp.int32, (CHUNK, 2 * CHUNK), 1) % CHUNK
        tri_t2 = (l2 >= s2).astype(F32)
        acs_t2 = lax.dot_general(adt, tri_t2, (((0,), (0,)), ((), ())),
                                 precision=lax.Precision.HIGHEST,
                                 preferred_element_type=F32)

        lane = lax.broadcasted_iota(jnp.int32, (CHUNK, 2 * CHUNK), 1)
        lo = lane < CHUNK
        lo_row = lo[0:1, :]
        tri2 = (lax.broadcasted_iota(jnp.int32, (CHUNK, 2 * CHUNK), 0) >= (lane % CHUNK))

        yoff = []
        cb2 = []
        for g in range(N_GROUPS):
            sl = slice(g * GROUP_W, (g + 1) * GROUP_W)
            ns = slice(g * D_STATE, (g + 1) * D_STATE)
            yoff.append(jnp.dot(cm_b[:, ns], ht_ref[:, sl].astype(BF16),
                                preferred_element_type=F32))
            b2 = jnp.concatenate([bm_b[:, ns], bm_b[:, ns]], axis=0)
            cb2.append(lax.dot_general(cm_b[:, ns], b2, (((1,), (1,)), ((), ())),
                                       preferred_element_type=F32))

        for j in range(SSM_HEADS // 2):
            g = j // (SSM_HEADS // 2 // N_GROUPS)
            r0, r1 = 2 * j, 2 * j + 1
            sl = slice(j * 128, (j + 1) * 128)
            gsl = slice((j % 4) * 128, (j % 4 + 1) * 128)
            u_col = jnp.where(lo, acs[:, r0:r0 + 1], acs[:, r1:r1 + 1])
            v_row = jnp.where(lo_row, acs_t2[r0:r0 + 1, :], acs_t2[r1:r1 + 1, :])
            dtp = jnp.where(lo, dt[:, r0:r0 + 1], dt[:, r1:r1 + 1])
            decay = jnp.exp2(jnp.where(tri2, u_col - v_row, -jnp.inf))
            m = (cb2[g] * decay).astype(BF16)
            xs_p = xs[:, sl]
            xdt = xs_p * dtp
            xdt_b = xdt.astype(BF16)
            zero = jnp.zeros_like(xdt_b)
            rhs = jnp.concatenate([jnp.where(lo, xdt_b, zero), jnp.where(lo, zero, xdt_b)], axis=0)
            yd = jnp.dot(m, rhs, preferred_element_type=F32)
            e_u = jnp.exp2(u_col)
            y = yd + yoff[g][:, gsl] * e_u + dskip_ref[:, sl] * xs_p
            to_end = jnp.exp2(u_col[CHUNK - 1:CHUNK, :] - u_col)
            xw_ref[:, sl] = (xdt * to_end).astype(BF16)
            ea_ref[:, sl] = e_u[CHUNK - 8:CHUNK, :]
            zf = z_ref[rows, sl].astype(F32)
            y_ref[:, sl] = y * _silu(zf)

        for g in range(N_GROUPS):
            sl = slice(g * GROUP_W, (g + 1) * GROUP_W)
            ns = slice(g * D_STATE, (g + 1) * D_STATE)
            yg = y_ref[:, sl]
            ms = jnp.mean(yg * yg, axis=-1, keepdims=True)
            yz_ref[rows, sl] = ((yg * lax.rsqrt(ms + NORM_EPS)) * nw_ref[:, sl]).astype(BF16)
            upd = lax.dot_general(bm_b[:, ns], xw_ref[:, sl], (((0,), (0,)), ((), ())),
                                  preferred_element_type=F32)
            ht_ref[:, sl] = ht_ref[:, sl] * ea_ref[7:8, sl] + upd

    if nsub == 1:
        chunk(slice(None))
    else:
        def sub(i, carry):
            chunk(pl.ds(pl.multiple_of(i * CHUNK, CHUNK), CHUNK))
            return carry
        lax.fori_loop(0, nsub, sub, 0)

    @pl.when(c == n_chunks - 1)
    def _():
        for g in range(N_GROUPS):
            sl = slice(g * GROUP_W, (g + 1) * GROUP_W)
            hout_ref[0, sl, :] = ht_ref[:, sl].T
        tailout_ref[0] = tail_ref[...]


def _ssd(p, dtraw, h0, conv0, conv_w, conv_b, dt_bias, a_log, d_skip_x, ssm_norm_w, nb, nc):
    n = nb * nc * CHUNK
    nsub = SSD_SUB if nc % SSD_SUB == 0 else 1
    ns = nc // nsub
    tr = nsub * CHUNK
    row = lambda b, c: b * ns + c
    const = lambda b, c: (0, 0)
    assert h0.shape[0] == conv0.shape[0] and h0.shape[0] in (1, nb)
    state_row = (lambda b: b) if h0.shape[0] == nb else (lambda b: 0)
    return pl.pallas_call(
        functools.partial(_ssd_kernel, nsub=nsub),
        grid=(nb, ns),
        in_specs=[
            pl.BlockSpec((tr, D_INNER), lambda b, c: (row(b, c), COL_Z // D_INNER)),
            pl.BlockSpec((tr, D_INNER), lambda b, c: (row(b, c), COL_X // D_INNER)),
            pl.BlockSpec((tr, BC_W), lambda b, c: (row(b, c), COL_B // BC_W)),
            pl.BlockSpec((tr, BC_W), lambda b, c: (row(b, c), COL_C // BC_W)),
            pl.BlockSpec((tr, DT_PAD), lambda b, c: (row(b, c), 0)),
            pl.BlockSpec((1, D_INNER, D_STATE), lambda b, c: (state_row(b), 0, 0)),
            pl.BlockSpec((1, 8, CONV_DIM), lambda b, c: (state_row(b), 0, 0)),
            pl.BlockSpec((CONV_W, CONV_DIM), const),
            pl.BlockSpec((1, CONV_DIM), const),
            pl.BlockSpec((1, DT_PAD), const),
            pl.BlockSpec((1, DT_PAD), const),
            pl.BlockSpec((1, D_INNER), const),
            pl.BlockSpec((1, D_INNER), const),
        ],
        out_specs=[
            pl.BlockSpec((tr, D_INNER), lambda b, c: (row(b, c), 0)),
            pl.BlockSpec((1, D_INNER, D_STATE), lambda b, c: (b, 0, 0)),
            pl.BlockSpec((1, 8, CONV_DIM), lambda b, c: (b, 0, 0)),
        ],
        out_shape=[
            jax.ShapeDtypeStruct((n, D_INNER), BF16),
            jax.ShapeDtypeStruct((nb, D_INNER, D_STATE), F32),
            jax.ShapeDtypeStruct((nb, 8, CONV_DIM), F32),
        ],
        scratch_shapes=[
            pltpu.VMEM((D_STATE, D_INNER), F32),
            pltpu.VMEM((8, CONV_DIM), F32),
            pltpu.VMEM((CHUNK, D_INNER), BF16),
            pltpu.VMEM((CHUNK, D_INNER), F32),
            pltpu.VMEM((8, D_INNER), F32),
        ],
        compiler_params=pltpu.CompilerParams(
            dimension_semantics=("arbitrary", "arbitrary"),
            vmem_limit_bytes=VMEM_LIMIT),
        name="ssd",
    )(p, p, p, p, dtraw, h0, conv0, conv_w, conv_b, dt_bias, a_log, d_skip_x, ssm_norm_w)


ATT_TQ = 256
ATT_TILES = 3
ATT_KEYS = ATT_TILES * ATT_TQ
BIAS_ROW = 1024
KEEP_TILES = LEFT_CHUNKS * CHUNK // ATT_TQ


def _bias_kernel(r_ref, o_ref):
    x = jnp.broadcast_to(r_ref[0], (ATT_TQ, BIAS_ROW))
    tab = pltpu.roll(x, BIAS_ROW - ATT_TQ, axis=1, stride=1, stride_axis=0)[:, :ATT_KEYS]
    qc = lax.broadcasted_iota(jnp.int32, (ATT_TQ, ATT_KEYS), 0) // CHUNK
    kc = lax.broadcasted_iota(jnp.int32, (ATT_TQ, ATT_KEYS), 1) // CHUNK
    ok = (kc >= qc) & (kc <= qc + LEFT_CHUNKS)
    o_ref[0] = jnp.where(ok, tab * LOG2E, NEG_INF)


def _bias_table(rel_row):
    return pl.pallas_call(
        _bias_kernel,
        grid=(ATT_HEADS,),
        in_specs=[pl.BlockSpec((1, 1, BIAS_ROW), lambda h: (h, 0, 0))],
        out_specs=pl.BlockSpec((1, ATT_TQ, ATT_KEYS), lambda h: (h, 0, 0)),
        out_shape=jax.ShapeDtypeStruct((ATT_HEADS, ATT_TQ, ATT_KEYS), F32),
        name="bias_table",
    )(rel_row)


def _softmax_pv(s_tiles, v_tiles):
    def lane_reduce(tiles, combine, reduce):
        if len({t.shape for t in tiles}) == 1:
            tiles = [functools.reduce(combine, tiles)]
        return functools.reduce(combine, [reduce(t, axis=-1, keepdims=True) for t in tiles])

    mx = lane_reduce(s_tiles, jnp.maximum, jnp.max)
    p_tiles = [jnp.exp2(s - mx) for s in s_tiles]
    den = lane_reduce(p_tiles, jnp.add, jnp.sum)
    o = functools.reduce(jnp.add, [jnp.dot(p.astype(BF16), v, preferred_element_type=F32)
                                   for p, v in zip(p_tiles, v_tiles)])
    return o / den


def _attn_prompt_kernel(q_ref, g_ref, kt0_ref, kt1_ref, kt2_ref, v0_ref, v1_ref, v2_ref,
                        bias_ref, o_ref, kp_ref, vp_ref):
    t = pl.program_id(1)
    kt_refs = (kt0_ref, kt1_ref, kt2_ref)
    v_refs = (v0_ref, v1_ref, v2_ref)

    @pl.when(t >= pl.num_programs(1) - KEEP_TILES)
    def _():
        kp_ref[0] = kt2_ref[...].astype(F32).T
        vp_ref[0] = v2_ref[...].astype(F32)

    def body(mask_start):
        lo = lax.broadcasted_iota(jnp.int32, (ATT_TQ, 2 * ATT_HEAD_DIM), 1) < ATT_HEAD_DIM
        for hp in range(ATT_HEADS // 2):
            sl = slice(hp * 128, (hp + 1) * 128)
            qp = q_ref[:, sl]
            zero = jnp.zeros_like(qp)
            q2 = jnp.concatenate([jnp.where(lo, qp, zero), jnp.where(lo, zero, qp)], axis=0)
            s_tiles = []
            for n in range(ATT_TILES):
                ks = slice(n * ATT_TQ, (n + 1) * ATT_TQ)
                s = jnp.dot(q2, kt_refs[n][sl, :], preferred_element_type=F32)
                s = s + jnp.concatenate([bias_ref[2 * hp, :, ks], bias_ref[2 * hp + 1, :, ks]],
                                        axis=0)
                if mask_start and n < ATT_TILES - 1:
                    s = jnp.where(t + n >= ATT_TILES - 1, s, NEG_INF)
                s_tiles.append(s)
            o2 = _softmax_pv(s_tiles, [v_refs[n][:, sl] for n in range(ATT_TILES)])
            o_pair = jnp.where(lo, o2[:ATT_TQ], o2[ATT_TQ:])
            gate = g_ref[:, sl].astype(F32)
            o_ref[:, sl] = (o_pair * _silu(gate)).astype(BF16)

    pl.when(t < ATT_TILES - 1)(functools.partial(body, True))
    pl.when(t >= ATT_TILES - 1)(functools.partial(body, False))


def _attn_prompt(p, kt, bias, nb, seq):
    nt = seq // ATT_TQ

    def tile_row(n):
        return lambda b, t: b * nt + jnp.maximum(t - (ATT_TILES - 1) + n, 0)

    in_specs = [
        pl.BlockSpec((ATT_TQ, D_ATT), lambda b, t: (b * nt + t, COL_Q // D_ATT)),
        pl.BlockSpec((ATT_TQ, D_ATT), lambda b, t: (b * nt + t, COL_G // D_ATT)),
    ]
    in_specs += [pl.BlockSpec((D_ATT, ATT_TQ), lambda b, t, r=tile_row(n): (0, r(b, t)))
                 for n in range(ATT_TILES)]
    in_specs += [pl.BlockSpec((ATT_TQ, D_ATT), lambda b, t, r=tile_row(n): (r(b, t), COL_V // D_ATT))
                 for n in range(ATT_TILES)]
    in_specs += [pl.BlockSpec((ATT_HEADS, ATT_TQ, ATT_KEYS), lambda b, t: (0, 0, 0),
                              pipeline_mode=pl.Buffered(1))]
    keep_spec = pl.BlockSpec((1, ATT_TQ, D_ATT),
                             lambda b, t: (b, jnp.maximum(t - (nt - KEEP_TILES), 0), 0))
    keep_shape = jax.ShapeDtypeStruct((nb, KEEP_TILES * ATT_TQ, D_ATT), F32)
    return pl.pallas_call(
        _attn_prompt_kernel,
        grid=(nb, nt),
        in_specs=in_specs,
        out_specs=[pl.BlockSpec((ATT_TQ, D_ATT), lambda b, t: (b * nt + t, 0)),
                   keep_spec, keep_spec],
        out_shape=[jax.ShapeDtypeStruct((nb * seq, D_ATT), BF16), keep_shape, keep_shape],
        compiler_params=pltpu.CompilerParams(
            dimension_semantics=("arbitrary", "arbitrary"),
            vmem_limit_bytes=VMEM_LIMIT),
        name="attn_prompt",
    )(p, p, kt, kt, kt, p, p, p, bias)


def _attn_step_kernel(q_ref, g_ref, kn_ref, vn_ref, kc_ref, vc_ref, bias_ref,
                      o_ref, ks_ref, vs_ref):
    keep = LEFT_CHUNKS * CHUNK
    ks_ref[0, 0:keep - CHUNK, :] = kc_ref[0, CHUNK:keep, :]
    ks_ref[0, keep - CHUNK:keep, :] = kn_ref[...].astype(F32)
    vs_ref[0, 0:keep - CHUNK, :] = vc_ref[0, CHUNK:keep, :]
    vs_ref[0, keep - CHUNK:keep, :] = vn_ref[...].astype(F32)
    lo = lax.broadcasted_iota(jnp.int32, (CHUNK, 2 * ATT_HEAD_DIM), 1) < ATT_HEAD_DIM

    for hp in range(ATT_HEADS // 2):
        sl = slice(hp * 128, (hp + 1) * 128)
        qp = q_ref[:, sl]
        k_tiles = [kc_ref[0, :, sl].astype(BF16), kn_ref[:, sl]]
        v_tiles = [vc_ref[0, :, sl].astype(BF16), vn_ref[:, sl]]
        zero = jnp.zeros_like(qp)
        q2 = jnp.concatenate([jnp.where(lo, qp, zero), jnp.where(lo, zero, qp)], axis=0)
        s_tiles = []
        for n, (k0, k1) in enumerate(((0, keep), (keep, BAND))):
            s = lax.dot_general(q2, k_tiles[n], (((1,), (1,)), ((), ())),
                                preferred_element_type=F32)
            s_tiles.append(s + jnp.concatenate([bias_ref[2 * hp, 0:CHUNK, k0:k1],
                                                bias_ref[2 * hp + 1, 0:CHUNK, k0:k1]], axis=0))
        o2 = _softmax_pv(s_tiles, v_tiles)
        o_pair = jnp.where(lo, o2[:CHUNK], o2[CHUNK:])
        gate = g_ref[:, sl].astype(F32)
        o_ref[:, sl] = (o_pair * _silu(gate)).astype(BF16)


def _attn_step(p, k_new, cache_k, cache_v, bias, nb):
    keep = LEFT_CHUNKS * CHUNK
    new = lambda col: pl.BlockSpec((CHUNK, D_ATT), lambda b: (b, col))
    cache = pl.BlockSpec((1, keep, D_ATT), lambda b: (b, 0, 0))
    cache_shape = jax.ShapeDtypeStruct((nb, keep, D_ATT), F32)
    return pl.pallas_call(
        _attn_step_kernel,
        grid=(nb,),
        in_specs=[new(COL_Q // D_ATT), new(COL_G // D_ATT), new(0),
                  new(COL_V // D_ATT), cache, cache,
                  pl.BlockSpec((ATT_HEADS, ATT_TQ, ATT_KEYS), lambda b: (0, 0, 0),
                               pipeline_mode=pl.Buffered(1))],
        out_specs=[pl.BlockSpec((CHUNK, D_ATT), lambda b: (b, 0)), cache, cache],
        out_shape=[jax.ShapeDtypeStruct((nb * CHUNK, D_ATT), BF16), cache_shape, cache_shape],
        compiler_params=pltpu.CompilerParams(
            dimension_semantics=("arbitrary",),
            vmem_limit_bytes=VMEM_LIMIT),
        name="attn_step",
    )(p, p, k_new, p, cache_k, cache_v, bias)


OUT_TM = 1024


def _out_kernel(yz_ref, og_ref, gs_ref, ga_ref, x_ref, wos_ref, woa_ref, wo_ref, fnw_ref, y_ref):
    y_ssm = jnp.dot(yz_ref[...], wos_ref[...], preferred_element_type=F32)
    y_att = jnp.dot(og_ref[...], woa_ref[...], preferred_element_type=F32)
    merged = (_sigmoid(gs_ref[...].astype(F32)) * y_ssm
              + _sigmoid(ga_ref[...].astype(F32)) * y_att)
    h = x_ref[...] + jnp.dot(merged.astype(BF16), wo_ref[...], preferred_element_type=F32)
    ms = jnp.mean(h * h, axis=-1, keepdims=True)
    y_ref[...] = (h * lax.rsqrt(ms + NORM_EPS)) * fnw_ref[...]


def _out(yz, og, p, x2d, w_out_ssm, w_out_att, w_o, final_norm_w):
    n = x2d.shape[0]
    tm = min(OUT_TM, n)
    const = lambda i: (0, 0)
    return pl.pallas_call(
        _out_kernel,
        grid=(n // tm,),
        in_specs=[
            pl.BlockSpec((tm, D_INNER), lambda i: (i, 0)),
            pl.BlockSpec((tm, D_ATT), lambda i: (i, 0)),
            pl.BlockSpec((tm, D_MODEL), lambda i: (i, COL_GS // D_MODEL)),
            pl.BlockSpec((tm, D_MODEL), lambda i: (i, COL_GA // D_MODEL)),
            pl.BlockSpec((tm, D_MODEL), lambda i: (i, 0)),
            pl.BlockSpec((D_INNER, D_MODEL), const, pipeline_mode=pl.Buffered(1)),
            pl.BlockSpec((D_ATT, D_MODEL), const, pipeline_mode=pl.Buffered(1)),
            pl.BlockSpec((D_MODEL, D_MODEL), const, pipeline_mode=pl.Buffered(1)),
            pl.BlockSpec((1, D_MODEL), const),
        ],
        out_specs=pl.BlockSpec((tm, D_MODEL), lambda i: (i, 0)),
        out_shape=jax.ShapeDtypeStruct((n, D_MODEL), F32),
        compiler_params=pltpu.CompilerParams(
            dimension_semantics=("arbitrary",),
            vmem_limit_bytes=VMEM_LIMIT),
        name="out",
    )(yz, og, p, p, x2d, w_out_ssm, w_out_att, w_o, final_norm_w)


def _rel_bias_rows(rel_bias):
    d = ATT_TILES * ATT_TQ - jnp.arange(BIAS_ROW)
    idx = jnp.clip(d, -REL_CLIP, REL_CLIP) + REL_CLIP
    return rel_bias[:, idx].astype(F32).reshape(ATT_HEADS, 1, BIAS_ROW)


def kernel(x_prompt, x_sample, state_ssm, state_conv, cache_k, cache_v, norm_w, w_in, conv_w,
           conv_b, dt_bias, a_log, d_skip, ssm_norm_w, w_out_ssm, rel_bias, w_out_att, w_o,
           final_norm_w):
    bp, lp, _ = x_prompt.shape
    bs, ls, _ = x_sample.shape
    ncp = lp // CHUNK
    assert ls == CHUNK and cache_k.shape[2] == LEFT_CHUNKS * CHUNK

    wt = w_in[0].T
    o = 0
    parts = {}
    for name, size in (("z", D_INNER), ("xbc", CONV_DIM), ("dt", SSM_HEADS), ("q", D_ATT),
                       ("k", D_ATT), ("v", D_ATT), ("g", D_ATT), ("gs", D_MODEL), ("ga", D_MODEL)):
        parts[name] = wt[o:o + size]
        o += size
    w_main = jnp.concatenate(
        [parts["z"], parts["xbc"], parts["q"] * (ATT_HEAD_DIM ** -0.5 * LOG2E), parts["v"],
         parts["g"], parts["gs"], parts["ga"]], axis=0).astype(BF16)
    w_kt = parts["k"].astype(BF16)
    w_dt = jnp.pad(parts["dt"].T, ((0, 0), (0, DT_PAD - SSM_HEADS))).astype(BF16)
    nw = norm_w[0].reshape(1, D_MODEL)
    cw = conv_w[0]
    cb = conv_b[0].reshape(1, CONV_DIM)
    dtb = jnp.pad(dt_bias[0], (0, DT_PAD - SSM_HEADS)).reshape(1, DT_PAD)
    alog = jnp.pad(a_log[0], (0, DT_PAD - SSM_HEADS)).reshape(1, DT_PAD)
    dskip_x = jnp.repeat(d_skip[0], SSM_HEAD_DIM).reshape(1, D_INNER)
    snw = ssm_norm_w[0].reshape(1, D_INNER)
    wos = w_out_ssm[0].astype(BF16)
    woa = w_out_att[0].astype(BF16)
    wo = w_o[0].astype(BF16)
    fnw = final_norm_w.reshape(1, D_MODEL)
    bias = _bias_table(_rel_bias_rows(rel_bias[0]))

    ssd_params = (cw, cb, dtb, alog, dskip_x, snw)
    out_params = (wos, woa, wo, fnw)

    xp = x_prompt.reshape(bp * lp, D_MODEL)
    p_p, dt_p, kt_p = _proj(xp, nw, w_main, w_dt, w_kt)
    h0_p = jnp.zeros((1, D_INNER, D_STATE), F32)
    conv0_p = jnp.zeros((1, 8, CONV_DIM), F32)
    yz_p, ssm_p, tail_p = _ssd(p_p, dt_p, h0_p, conv0_p, *ssd_params, nb=bp, nc=ncp)
    og_p, k_p, v_p = _attn_prompt(p_p, kt_p, bias, nb=bp, seq=lp)
    y_p = _out(yz_p, og_p, p_p, xp, *out_params).reshape(bp, lp, D_MODEL)
    keep = LEFT_CHUNKS * CHUNK
    k_p = k_p.reshape(1, bp, keep, ATT_HEADS, ATT_HEAD_DIM)
    v_p = v_p.reshape(1, bp, keep, ATT_HEADS, ATT_HEAD_DIM)

    xs = x_sample.reshape(bs * ls, D_MODEL)
    p_s, dt_s, kt_s = _proj(xs, nw, w_main, w_dt, w_kt)
    h0_s = state_ssm[0].reshape(bs, D_INNER, D_STATE)
    conv0_s = jnp.pad(state_conv[0], ((0, 0), (8 - (CONV_W - 1), 0), (0, 0)))
    yz_s, ssm_s, tail_s = _ssd(p_s, dt_s, h0_s, conv0_s, *ssd_params, nb=bs, nc=1)
    og_s, k_s, v_s = _attn_step(p_s, kt_s.T, cache_k[0].reshape(bs, keep, D_ATT),
                                cache_v[0].reshape(bs, keep, D_ATT), bias, nb=bs)
    y_s = _out(yz_s, og_s, p_s, xs, *out_params).reshape(bs, ls, D_MODEL)
    k_s = k_s.reshape(1, bs, keep, ATT_HEADS, ATT_HEAD_DIM)
    v_s = v_s.reshape(1, bs, keep, ATT_HEADS, ATT_HEAD_DIM)

    def states(ssm, tail, nb):
        return (ssm.reshape(1, nb, SSM_HEADS, SSM_HEAD_DIM, D_STATE),
                tail[:, 8 - (CONV_W - 1):, :].reshape(1, nb, CONV_W - 1, CONV_DIM))

    ssm_p5, conv_p4 = states(ssm_p, tail_p, bp)
    ssm_s5, conv_s4 = states(ssm_s, tail_s, bs)
    return (y_p, y_s, ssm_p5, conv_p4, k_p, v_p, ssm_s5, conv_s4, k_s, v_s)
```

```python
import functools

import jax
import jax.numpy as jnp
from jax import lax
from jax.experimental import pallas as pl
from jax.experimental.pallas import tpu as pltpu

D_MODEL = 1024
CHUNK = 64
D_INNER = 2048
SSM_HEAD_DIM = 64
SSM_HEADS = 32
N_GROUPS = 4
D_STATE = 128
GROUP_W = D_INNER // N_GROUPS
BC_W = N_GROUPS * D_STATE
CONV_W = 4
CONV_DIM = D_INNER + 2 * BC_W
ATT_HEADS = 16
ATT_HEAD_DIM = 64
D_ATT = 1024
LEFT_CHUNKS = 8
BAND_CHUNKS = LEFT_CHUNKS + 1
BAND = BAND_CHUNKS * CHUNK
REL_CLIP = 256
NORM_EPS = 1e-5
NEG_INF = -1e30
LOG2E = 1.4426950408889634

COL_Z = 0
COL_X = 2048
COL_B = 4096
COL_C = 4608
COL_Q = 5120
COL_V = 6144
COL_G = 7168
COL_GS = 8192
COL_GA = 9216
P_COLS = 10240
DT_PAD = 128

VMEM_LIMIT = 56 * 1024 * 1024

F32 = jnp.float32
BF16 = jnp.bfloat16


def _sigmoid(x):
    return 0.5 * jnp.tanh(0.5 * x) + 0.5


def _silu(x):
    h = 0.5 * x
    return h * jnp.tanh(h) + h


def _softplus(x):
    return jnp.maximum(x, 0.0) + jnp.log(1.0 + jnp.exp(-jnp.abs(x)))


PROJ_TM = 1024
PROJ_TN = 2560


def _proj_kernel(x_ref, nw_ref, wt_ref, wdt_ref, wkt_ref, p_ref, dt_ref, kt_ref, xn_ref):
    @pl.when(pl.program_id(1) == 0)
    def _():
        x = x_ref[...]
        ms = jnp.mean(x * x, axis=-1, keepdims=True)
        xn = (x * lax.rsqrt(ms + NORM_EPS)) * nw_ref[...]
        xn_ref[...] = xn.astype(BF16)
        dt_ref[...] = jnp.dot(xn_ref[...], wdt_ref[...], preferred_element_type=F32)
        kt_ref[...] = lax.dot_general(wkt_ref[...], xn_ref[...], (((1,), (1,)), ((), ())),
                                      preferred_element_type=F32).astype(BF16)

    p_ref[...] = lax.dot_general(xn_ref[...], wt_ref[...], (((1,), (1,)), ((), ())),
                                 preferred_element_type=F32).astype(BF16)


def _proj(x2d, norm_w, w_main, w_dt, w_kt):
    n = x2d.shape[0]
    tm = min(PROJ_TM, n)
    return pl.pallas_call(
        _proj_kernel,
        grid=(n // tm, P_COLS // PROJ_TN),
        in_specs=[
            pl.BlockSpec((tm, D_MODEL), lambda i, j: (i, 0)),
            pl.BlockSpec((1, D_MODEL), lambda i, j: (0, 0)),
            pl.BlockSpec((PROJ_TN, D_MODEL), lambda i, j: (j, 0)),
            pl.BlockSpec((D_MODEL, DT_PAD), lambda i, j: (0, 0)),
            pl.BlockSpec((D_ATT, D_MODEL), lambda i, j: (0, 0)),
        ],
        out_specs=[
            pl.BlockSpec((tm, PROJ_TN), lambda i, j: (i, j)),
            pl.BlockSpec((tm, DT_PAD), lambda i, j: (i, 0)),
            pl.BlockSpec((D_ATT, tm), lambda i, j: (0, i)),
        ],
        out_shape=[
            jax.ShapeDtypeStruct((n, P_COLS), BF16),
            jax.ShapeDtypeStruct((n, DT_PAD), F32),
            jax.ShapeDtypeStruct((D_ATT, n), BF16),
        ],
        scratch_shapes=[pltpu.VMEM((tm, D_MODEL), BF16)],
        compiler_params=pltpu.CompilerParams(
            dimension_semantics=("arbitrary", "arbitrary"),
            vmem_limit_bytes=VMEM_LIMIT),
        name="proj",
    )(x2d, norm_w, w_main, w_dt, w_kt)


SSD_SUB = 16


def _shift_matrix():
    t = lax.broadcasted_iota(jnp.int32, (CHUNK, (CONV_W - 1) * CHUNK), 0)
    c = lax.broadcasted_iota(jnp.int32, (CHUNK, (CONV_W - 1) * CHUNK), 1)
    return (c % CHUNK == t - (c // CHUNK + 1)).astype(BF16)


def _conv_silu(u_bf16, tail_ref, cols, w, b, shift_mat):
    u = u_bf16.astype(F32)
    taps = [w[CONV_W - 1 - s:CONV_W - s, :] for s in range(CONV_W)]
    delayed = jnp.concatenate([(taps[s] * u).astype(BF16) for s in range(1, CONV_W)], axis=0)
    acc = b + taps[0] * u + jnp.dot(shift_mat, delayed, preferred_element_type=F32)
    tail = tail_ref[:, cols]
    row8 = lax.broadcasted_iota(jnp.int32, (8, 1), 0)
    head = jnp.zeros_like(tail)
    for s in range(1, CONV_W):
        head = head + jnp.where(row8 < s, taps[s] * pltpu.roll(tail, s, axis=0), 0.0)
    acc = jnp.concatenate([acc[0:8, :] + head, acc[8:, :]], axis=0)
    tail_ref[:, cols] = u[CHUNK - 8:, :]
    return _silu(acc)


def _ssd_kernel(z_ref, x_ref, b_ref, c_ref, dtr_ref, h0_ref, conv0_ref,
                cw_ref, cb_ref, dtb_ref, alog_ref, dskip_ref, nw_ref,
                yz_ref, hout_ref, tailout_ref,
                ht_ref, tail_ref, xw_ref, y_ref, ea_ref, *, nsub):
    c = pl.program_id(1)
    n_chunks = pl.num_programs(1)

    @pl.when(c == 0)
    def _():
        for g in range(N_GROUPS):
            sl = slice(g * GROUP_W, (g + 1) * GROUP_W)
            ht_ref[:, sl] = h0_ref[0, sl, :].T
        tail_ref[...] = conv0_ref[0]

    def chunk(rows):
        cx = slice(0, D_INNER)
        cbm = slice(D_INNER, D_INNER + BC_W)
        ccm = slice(D_INNER + BC_W, CONV_DIM)
        shift_mat = _shift_matrix()
        xs = _conv_silu(x_ref[rows, :], tail_ref, cx, cw_ref[:, cx], cb_ref[:, cx], shift_mat)
        bm = _conv_silu(b_ref[rows, :], tail_ref, cbm, cw_ref[:, cbm], cb_ref[:, cbm], shift_mat)
        cm = _conv_silu(c_ref[rows, :], tail_ref, ccm, cw_ref[:, ccm], cb_ref[:, ccm], shift_mat)
        bm_b = bm.astype(BF16)
        cm_b = cm.astype(BF16)

        dt = _softplus(dtr_ref[rows, :] + dtb_ref[...])
        adt = dt * (-jnp.exp(alog_ref[...]) * LOG2E)
        li = lax.broadcasted_iota(jnp.int32, (CHUNK, CHUNK), 0)
        si = lax.broadcasted_iota(jnp.int32, (CHUNK, CHUNK), 1)
        tri = (li >= si).astype(F32)
        acs = jnp.dot(tri, adt, precision=lax.Precision.HIGHEST,
                      preferred_element_type=F32)
        s2 = lax.broadcasted_iota(jnp.int32, (CHUNK, 2 * CHUNK), 0)
        l2 = lax.broadcasted_iota(jnp.int32, (CHUNK, 2 * CHUNK), 1) % CHUNK
        tri_t2 = (l2 >= s2).astype(F32)
        acs_t2 = lax.dot_general(adt, tri_t2, (((0,), (0,)), ((), ())),
                                 precision=lax.Precision.HIGHEST,
                                 preferred_element_type=F32)

        lane = lax.broadcasted_iota(jnp.int32, (CHUNK, 2 * CHUNK), 1)
        lo = lane < CHUNK
        lo_row = lo[0:1, :]
        tri2 = (lax.broadcasted_iota(jnp.int32, (CHUNK, 2 * CHUNK), 0) >= (lane % CHUNK))

        yoff = []
        cb2 = []
        for g in range(N_GROUPS):
            sl = slice(g * GROUP_W, (g + 1) * GROUP_W)
            ns = slice(g * D_STATE, (g + 1) * D_STATE)
            yoff.append(jnp.dot(cm_b[:, ns], ht_ref[:, sl].astype(BF16),
                                preferred_element_type=F32))
            b2 = jnp.concatenate([bm_b[:, ns], bm_b[:, ns]], axis=0)
            cb2.append(lax.dot_general(cm_b[:, ns], b2, (((1,), (1,)), ((), ())),
                                       preferred_element_type=F32))

        for j in range(SSM_HEADS // 2):
            g = j // (SSM_HEADS // 2 // N_GROUPS)
            r0, r1 = 2 * j, 2 * j + 1
            sl = slice(j * 128, (j + 1) * 128)
            gsl = slice((j % 4) * 128, (j % 4 + 1) * 128)
            u_col = jnp.where(lo, acs[:, r0:r0 + 1], acs[:, r1:r1 + 1])
            v_row = jnp.where(lo_row, acs_t2[r0:r0 + 1, :], acs_t2[r1:r1 + 1, :])
            dtp = jnp.where(lo, dt[:, r0:r0 + 1], dt[:, r1:r1 + 1])
            decay = jnp.exp2(jnp.where(tri2, u_col - v_row, -jnp.inf))
            m = (cb2[g] * decay).astype(BF16)
            xs_p = xs[:, sl]
            xdt = xs_p * dtp
            xdt_b = xdt.astype(BF16)
            zero = jnp.zeros_like(xdt_b)
            rhs = jnp.concatenate([jnp.where(lo, xdt_b, zero), jnp.where(lo, zero, xdt_b)], axis=0)
            yd = jnp.dot(m, rhs, preferred_element_type=F32)
            e_u = jnp.exp2(u_col)
            y = yd + yoff[g][:, gsl] * e_u + dskip_ref[:, sl] * xs_p
            to_end = jnp.exp2(u_col[CHUNK - 1:CHUNK, :] - u_col)
            xw_ref[:, sl] = (xdt * to_end).astype(BF16)
            ea_ref[:, sl] = e_u[CHUNK - 8:CHUNK, :]
            zf = z_ref[rows, sl].astype(F32)
            y_ref[:, sl] = y * _silu(zf)

        for g in range(N_GROUPS):
            sl = slice(g * GROUP_W, (g + 1) * GROUP_W)
            ns = slice(g * D_STATE, (g + 1) * D_STATE)
            yg = y_ref[:, sl]
            ms = jnp.mean(yg * yg, axis=-1, keepdims=True)
            yz_ref[rows, sl] = ((yg * lax.rsqrt(ms + NORM_EPS)) * nw_ref[:, sl]).astype(BF16)
            upd = lax.dot_general(bm_b[:, ns], xw_ref[:, sl], (((0,), (0,)), ((), ())),
                                  preferred_element_type=F32)
            ht_ref[:, sl] = ht_ref[:, sl] * ea_ref[7:8, sl] + upd

    if nsub == 1:
        chunk(slice(None))
    else:
        def sub(i, carry):
            chunk(pl.ds(pl.multiple_of(i * CHUNK, CHUNK), CHUNK))
            return carry
        lax.fori_loop(0, nsub, sub, 0)

    @pl.when(c == n_chunks - 1)
    def _():
        for g in range(N_GROUPS):
            sl = slice(g * GROUP_W, (g + 1) * GROUP_W)
            hout_ref[0, sl, :] = ht_ref[:, sl].T
        tailout_ref[0] = tail_ref[...]


def _ssd(p, dtraw, h0, conv0, conv_w, conv_b, dt_bias, a_log, d_skip_x, ssm_norm_w, nb, nc):
    n = nb * nc * CHUNK
    nsub = SSD_SUB if nc % SSD_SUB == 0 else 1
    ns = nc // nsub
    tr = nsub * CHUNK
    row = lambda b, c: b * ns + c
    const = lambda b, c: (0, 0)
    assert h0.shape[0] == conv0.shape[0] and h0.shape[0] in (1, nb)
    state_row = (lambda b: b) if h0.shape[0] == nb else (lambda b: 0)
    return pl.pallas_call(
        functools.partial(_ssd_kernel, nsub=nsub),
        grid=(nb, ns),
        in_specs=[
            pl.BlockSpec((tr, D_INNER), lambda b, c: (row(b, c), COL_Z // D_INNER)),
            pl.BlockSpec((tr, D_INNER), lambda b, c: (row(b, c), COL_X // D_INNER)),
            pl.BlockSpec((tr, BC_W), lambda b, c: (row(b, c), COL_B // BC_W)),
            pl.BlockSpec((tr, BC_W), lambda b, c: (row(b, c), COL_C // BC_W)),
            pl.BlockSpec((tr, DT_PAD), lambda b, c: (row(b, c), 0)),
            pl.BlockSpec((1, D_INNER, D_STATE), lambda b, c: (state_row(b), 0, 0)),
            pl.BlockSpec((1, 8, CONV_DIM), lambda b, c: (state_row(b), 0, 0)),
            pl.BlockSpec((CONV_W, CONV_DIM), const),
            pl.BlockSpec((1, CONV_DIM), const),
            pl.BlockSpec((1, DT_PAD), const),
            pl.BlockSpec((1, DT_PAD), const),
            pl.BlockSpec((1, D_INNER), const),
            pl.BlockSpec((1, D_INNER), const),
        ],
        out_specs=[
            pl.BlockSpec((tr, D_INNER), lambda b, c: (row(b, c), 0)),
            pl.BlockSpec((1, D_INNER, D_STATE), lambda b, c: (b, 0, 0)),
            pl.BlockSpec((1, 8, CONV_DIM), lambda b, c: (b, 0, 0)),
        ],
        out_shape=[
            jax.ShapeDtypeStruct((n, D_INNER), BF16),
            jax.ShapeDtypeStruct((nb, D_INNER, D_STATE), F32),
            jax.ShapeDtypeStruct((nb, 8, CONV_DIM), F32),
        ],
        scratch_shapes=[
            pltpu.VMEM((D_STATE, D_INNER), F32),
            pltpu.VMEM((8, CONV_DIM), F32),
            pltpu.VMEM((CHUNK, D_INNER), BF16),
            pltpu.VMEM((CHUNK, D_INNER), F32),
            pltpu.VMEM((8, D_INNER), F32),
        ],
        compiler_params=pltpu.CompilerParams(
            dimension_semantics=("arbitrary", "arbitrary"),
            vmem_limit_bytes=VMEM_LIMIT),
        name="ssd",
    )(p, p, p, p, dtraw, h0, conv0, conv_w, conv_b, dt_bias, a_log, d_skip_x, ssm_norm_w)


ATT_TQ = 256
ATT_TILES = 3
ATT_KEYS = ATT_TILES * ATT_TQ
BIAS_ROW = 1024
KEEP_TILES = LEFT_CHUNKS * CHUNK // ATT_TQ


def _bias_kernel(r_ref, o_ref):
    x = jnp.broadcast_to(r_ref[0], (ATT_TQ, BIAS_ROW))
    tab = pltpu.roll(x, BIAS_ROW - ATT_TQ, axis=1, stride=1, stride_axis=0)[:, :ATT_KEYS]
    qc = lax.broadcasted_iota(jnp.int32, (ATT_TQ, ATT_KEYS), 0) // CHUNK
    kc = lax.broadcasted_iota(jnp.int32, (ATT_TQ, ATT_KEYS), 1) // CHUNK
    ok = (kc >= qc) & (kc <= qc + LEFT_CHUNKS)
    o_ref[0] = jnp.where(ok, tab * LOG2E, NEG_INF)


def _bias_table(rel_row):
    return pl.pallas_call(
        _bias_kernel,
        grid=(ATT_HEADS,),
        in_specs=[pl.BlockSpec((1, 1, BIAS_ROW), lambda h: (h, 0, 0))],
        out_specs=pl.BlockSpec((1, ATT_TQ, ATT_KEYS), lambda h: (h, 0, 0)),
        out_shape=jax.ShapeDtypeStruct((ATT_HEADS, ATT_TQ, ATT_KEYS), F32),
        name="bias_table",
    )(rel_row)


def _softmax_pv(s_tiles, v_tiles):
    def lane_reduce(tiles, combine, reduce):
        if len({t.shape for t in tiles}) == 1:
            tiles = [functools.reduce(combine, tiles)]
        return functools.reduce(combine, [reduce(t, axis=-1, keepdims=True) for t in tiles])

    mx = lane_reduce(s_tiles, jnp.maximum, jnp.max)
    p_tiles = [jnp.exp2(s - mx) for s in s_tiles]
    den = lane_reduce(p_tiles, jnp.add, jnp.sum)
    o = functools.reduce(jnp.add, [jnp.dot(p.astype(BF16), v, preferred_element_type=F32)
                                   for p, v in zip(p_tiles, v_tiles)])
    return o / den


def _attn_prompt_kernel(q_ref, g_ref, kt0_ref, kt1_ref, kt2_ref, v0_ref, v1_ref, v2_ref,
                        bias_ref, o_ref, kp_ref, vp_ref):
    t = pl.program_id(1)
    kt_refs = (kt0_ref, kt1_ref, kt2_ref)
    v_refs = (v0_ref, v1_ref, v2_ref)

    @pl.when(t >= pl.num_programs(1) - KEEP_TILES)
    def _():
        kp_ref[0] = kt2_ref[...].astype(F32).T
        vp_ref[0] = v2_ref[...].astype(F32)

    def body(mask_start):
        lo = lax.broadcasted_iota(jnp.int32, (ATT_TQ, 2 * ATT_HEAD_DIM), 1) < ATT_HEAD_DIM
        for hp in range(ATT_HEADS // 2):
            sl = slice(hp * 128, (hp + 1) * 128)
            qp = q_ref[:, sl]
            zero = jnp.zeros_like(qp)
            q2 = jnp.concatenate([jnp.where(lo, qp, zero), jnp.where(lo, zero, qp)], axis=0)
            s_tiles = []
            for n in range(ATT_TILES):
                ks = slice(n * ATT_TQ, (n + 1) * ATT_TQ)
                s = jnp.dot(q2, kt_refs[n][sl, :], preferred_element_type=F32)
                s = s + jnp.concatenate([bias_ref[2 * hp, :, ks], bias_ref[2 * hp + 1, :, ks]],
                                        axis=0)
                if mask_start and n < ATT_TILES - 1:
                    s = jnp.where(t + n >= ATT_TILES - 1, s, NEG_INF)
                s_tiles.append(s)
            o2 = _softmax_pv(s_tiles, [v_refs[n][:, sl] for n in range(ATT_TILES)])
            o_pair = jnp.where(lo, o2[:ATT_TQ], o2[ATT_TQ:])
            gate = g_ref[:, sl].astype(F32)
            o_ref[:, sl] = (o_pair * _silu(gate)).astype(BF16)

    pl.when(t < ATT_TILES - 1)(functools.partial(body, True))
    pl.when(t >= ATT_TILES - 1)(functools.partial(body, False))


def _attn_prompt(p, kt, bias, nb, seq):
    nt = seq // ATT_TQ

    def tile_row(n):
        return lambda b, t: b * nt + jnp.maximum(t - (ATT_TILES - 1) + n, 0)

    in_specs = [
        pl.BlockSpec((ATT_TQ, D_ATT), lambda b, t: (b * nt + t, COL_Q // D_ATT)),
        pl.BlockSpec((ATT_TQ, D_ATT), lambda b, t: (b * nt + t, COL_G // D_ATT)),
    ]
    in_specs += [pl.BlockSpec((D_ATT, ATT_TQ), lambda b, t, r=tile_row(n): (0, r(b, t)))
                 for n in range(ATT_TILES)]
    in_specs += [pl.BlockSpec((ATT_TQ, D_ATT), lambda b, t, r=tile_row(n): (r(b, t), COL_V // D_ATT))
                 for n in range(ATT_TILES)]
    in_specs += [pl.BlockSpec((ATT_HEADS, ATT_TQ, ATT_KEYS), lambda b, t: (0, 0, 0),
                              pipeline_mode=pl.Buffered(1))]
    keep_spec = pl.BlockSpec((1, ATT_TQ, D_ATT),
                             lambda b, t: (b, jnp.maximum(t - (nt - KEEP_TILES), 0), 0))
    keep_shape = jax.ShapeDtypeStruct((nb, KEEP_TILES * ATT_TQ, D_ATT), F32)
    return pl.pallas_call(
        _attn_prompt_kernel,
        grid=(nb, nt),
        in_specs=in_specs,
        out_specs=[pl.BlockSpec((ATT_TQ, D_ATT), lambda b, t: (b * nt + t, 0)),
                   keep_spec, keep_spec],
        out_shape=[jax.ShapeDtypeStruct((nb * seq, D_ATT), BF16), keep_shape, keep_shape],
        compiler_params=pltpu.CompilerParams(
            dimension_semantics=("arbitrary", "arbitrary"),
            vmem_limit_bytes=VMEM_LIMIT),
        name="attn_prompt",
    )(p, p, kt, kt, kt, p, p, p, bias)


def _attn_step_kernel(q_ref, g_ref, kn_ref, vn_ref, kc_ref, vc_ref, bias_ref,
                      o_ref, ks_ref, vs_ref):
    keep = LEFT_CHUNKS * CHUNK
    ks_ref[0, 0:keep - CHUNK, :] = kc_ref[0, CHUNK:keep, :]
    ks_ref[0, keep - CHUNK:keep, :] = kn_ref[...].astype(F32)
    vs_ref[0, 0:keep - CHUNK, :] = vc_ref[0, CHUNK:keep, :]
    vs_ref[0, keep - CHUNK:keep, :] = vn_ref[...].astype(F32)
    lo = lax.broadcasted_iota(jnp.int32, (CHUNK, 2 * ATT_HEAD_DIM), 1) < ATT_HEAD_DIM

    for hp in range(ATT_HEADS // 2):
        sl = slice(hp * 128, (hp + 1) * 128)
        qp = q_ref[:, sl]
        k_tiles = [kc_ref[0, :, sl].astype(BF16), kn_ref[:, sl]]
        v_tiles = [vc_ref[0, :, sl].astype(BF16), vn_ref[:, sl]]
        zero = jnp.zeros_like(qp)
        q2 = jnp.concatenate([jnp.where(lo, qp, zero), jnp.where(lo, zero, qp)], axis=0)
        s_tiles = []
        for n, (k0, k1) in enumerate(((0, keep), (keep, BAND))):
            s = lax.dot_general(q2, k_tiles[n], (((1,), (1,)), ((), ())),
                                preferred_element_type=F32)
            s_tiles.append(s + jnp.concatenate([bias_ref[2 * hp, 0:CHUNK, k0:k1],
                                                bias_ref[2 * hp + 1, 0:CHUNK, k0:k1]], axis=0))
        o2 = _softmax_pv(s_tiles, v_tiles)
        o_pair = jnp.where(lo, o2[:CHUNK], o2[CHUNK:])
        gate = g_ref[:, sl].astype(F32)
        o_ref[:, sl] = (o_pair * _silu(gate)).astype(BF16)


def _attn_step(p, k_new, cache_k, cache_v, bias, nb):
    keep = LEFT_CHUNKS * CHUNK
    new = lambda col: pl.BlockSpec((CHUNK, D_ATT), lambda b: (b, col))
    cache = pl.BlockSpec((1, keep, D_ATT), lambda b: (b, 0, 0))
    cache_shape = jax.ShapeDtypeStruct((nb, keep, D_ATT), F32)
    return pl.pallas_call(
        _attn_step_kernel,
        grid=(nb,),
        in_specs=[new(COL_Q // D_ATT), new(COL_G // D_ATT), new(0),
                  new(COL_V // D_ATT), cache, cache,
                  pl.BlockSpec((ATT_HEADS, ATT_TQ, ATT_KEYS), lambda b: (0, 0, 0),
                               pipeline_mode=pl.Buffered(1))],
        out_specs=[pl.BlockSpec((CHUNK, D_ATT), lambda b: (b, 0)), cache, cache],
        out_shape=[jax.ShapeDtypeStruct((nb * CHUNK, D_ATT), BF16), cache_shape, cache_shape],
        compiler_params=pltpu.CompilerParams(
            dimension_semantics=("arbitrary",),
            vmem_limit_bytes=VMEM_LIMIT),
        name="attn_step",
    )(p, p, k_new, p, cache_k, cache_v, bias)


OUT_TM = 1024


def _out_kernel(yz_ref, og_ref, gs_ref, ga_ref, x_ref, wos_ref, woa_ref, wo_ref, fnw_ref, y_ref):
    y_ssm = jnp.dot(yz_ref[...], wos_ref[...], preferred_element_type=F32)
    y_att = jnp.dot(og_ref[...], woa_ref[...], preferred_element_type=F32)
    merged = (_sigmoid(gs_ref[...].astype(F32)) * y_ssm
              + _sigmoid(ga_ref[...].astype(F32)) * y_att)
    h = x_ref[...] + jnp.dot(merged.astype(BF16), wo_ref[...], preferred_element_type=F32)
    ms = jnp.mean(h * h, axis=-1, keepdims=True)
    y_ref[...] = (h * lax.rsqrt(ms + NORM_EPS)) * fnw_ref[...]


def _out(yz, og, p, x2d, w_out_ssm, w_out_att, w_o, final_norm_w):
    n = x2d.shape[0]
    tm = min(OUT_TM, n)
    const = lambda i: (0, 0)
    return pl.pallas_call(
        _out_kernel,
        grid=(n // tm,),
        in_specs=[
            pl.BlockSpec((tm, D_INNER), lambda i: (i, 0)),
            pl.BlockSpec((tm, D_ATT), lambda i: (i, 0)),
            pl.BlockSpec((tm, D_MODEL), lambda i: (i, COL_GS // D_MODEL)),
            pl.BlockSpec((tm, D_MODEL), lambda i: (i, COL_GA // D_MODEL)),
            pl.BlockSpec((tm, D_MODEL), lambda i: (i, 0)),
            pl.BlockSpec((D_INNER, D_MODEL), const, pipeline_mode=pl.Buffered(1)),
            pl.BlockSpec((D_ATT, D_MODEL), const, pipeline_mode=pl.Buffered(1)),
            pl.BlockSpec((D_MODEL, D_MODEL), const, pipeline_mode=pl.Buffered(1)),
            pl.BlockSpec((1, D_MODEL), const),
        ],
        out_specs=pl.BlockSpec((tm, D_MODEL), lambda i: (i, 0)),
        out_shape=jax.ShapeDtypeStruct((n, D_MODEL), F32),
        compiler_params=pltpu.CompilerParams(
            dimension_semantics=("arbitrary",),
            vmem_limit_bytes=VMEM_LIMIT),
        name="out",
    )(yz, og, p, p, x2d, w_out_ssm, w_out_att, w_o, final_norm_w)


def _rel_bias_rows(rel_bias):
    d = ATT_TILES * ATT_TQ - jnp.arange(BIAS_ROW)
    idx = jnp.clip(d, -REL_CLIP, REL_CLIP) + REL_CLIP
    return rel_bias[:, idx].astype(F32).reshape(ATT_HEADS, 1, BIAS_ROW)


def kernel(x_prompt, x_sample, state_ssm, state_conv, cache_k, cache_v, norm_w, w_in, conv_w,
           conv_b, dt_bias, a_log, d_skip, ssm_norm_w, w_out_ssm, rel_bias, w_out_att, w_o,
           final_norm_w):
    bp, lp, _ = x_prompt.shape
    bs, ls, _ = x_sample.shape
    ncp = lp // CHUNK
    assert ls == CHUNK and cache_k.shape[2] == LEFT_CHUNKS * CHUNK

    wt = w_in[0].T
    o = 0
    parts = {}
    for name, size in (("z", D_INNER), ("xbc", CONV_DIM), ("dt", SSM_HEADS), ("q", D_ATT),
                       ("k", D_ATT), ("v", D_ATT), ("g", D_ATT), ("gs", D_MODEL), ("ga", D_MODEL)):
        parts[name] = wt[o:o + size]
        o += size
    w_main = jnp.concatenate(
        [parts["z"], parts["xbc"], parts["q"] * (ATT_HEAD_DIM ** -0.5 * LOG2E), parts["v"],
         parts["g"], parts["gs"], parts["ga"]], axis=0).astype(BF16)
    w_kt = parts["k"].astype(BF16)
    w_dt = jnp.pad(parts["dt"].T, ((0, 0), (0, DT_PAD - SSM_HEADS))).astype(BF16)
    nw = norm_w[0].reshape(1, D_MODEL)
    cw = conv_w[0]
    cb = conv_b[0].reshape(1, CONV_DIM)
    dtb = jnp.pad(dt_bias[0], (0, DT_PAD - SSM_HEADS)).reshape(1, DT_PAD)
    alog = jnp.pad(a_log[0], (0, DT_PAD - SSM_HEADS)).reshape(1, DT_PAD)
    dskip_x = jnp.repeat(d_skip[0], SSM_HEAD_DIM).reshape(1, D_INNER)
    snw = ssm_norm_w[0].reshape(1, D_INNER)
    wos = w_out_ssm[0].astype(BF16)
    woa = w_out_att[0].astype(BF16)
    wo = w_o[0].astype(BF16)
    fnw = final_norm_w.reshape(1, D_MODEL)
    bias = _bias_table(_rel_bias_rows(rel_bias[0]))

    ssd_params = (cw, cb, dtb, alog, dskip_x, snw)
    out_params = (wos, woa, wo, fnw)

    xp = x_prompt.reshape(bp * lp, D_MODEL)
    p_p, dt_p, kt_p = _proj(xp, nw, w_main, w_dt, w_kt)
    h0_p = jnp.zeros((1, D_INNER, D_STATE), F32)
    conv0_p = jnp.zeros((1, 8, CONV_DIM), F32)
    yz_p, ssm_p, tail_p = _ssd(p_p, dt_p, h0_p, conv0_p, *ssd_params, nb=bp, nc=ncp)
    og_p, k_p, v_p = _attn_prompt(p_p, kt_p, bias, nb=bp, seq=lp)
    y_p = _out(yz_p, og_p, p_p, xp, *out_params).reshape(bp, lp, D_MODEL)
    keep = LEFT_CHUNKS * CHUNK
    k_p = k_p.reshape(1, bp, keep, ATT_HEADS, ATT_HEAD_DIM)
    v_p = v_p.reshape(1, bp, keep, ATT_HEADS, ATT_HEAD_DIM)

    xs = x_sample.reshape(bs * ls, D_MODEL)
    p_s, dt_s, kt_s = _proj(xs, nw, w_main, w_dt, w_kt)
    h0_s = state_ssm[0].reshape(bs, D_INNER, D_STATE)
    conv0_s = jnp.pad(state_conv[0], ((0, 0), (8 - (CONV_W - 1), 0), (0, 0)))
    yz_s, ssm_s, tail_s = _ssd(p_s, dt_s, h0_s, conv0_s, *ssd_params, nb=bs, nc=1)
    og_s, k_s, v_s = _attn_step(p_s, kt_s.T, cache_k[0].reshape(bs, keep, D_ATT),
                                cache_v[0].reshape(bs, keep, D_ATT), bias, nb=bs)
    y_s = _out(yz_s, og_s, p_s, xs, *out_params).reshape(bs, ls, D_MODEL)
    k_s = k_s.reshape(1, bs, keep, ATT_HEADS, ATT_HEAD_DIM)
    v_s = v_s.reshape(1, bs, keep, ATT_HEADS, ATT_HEAD_DIM)

    def states(ssm, tail, nb):
        return (ssm.reshape(1, nb, SSM_HEADS, SSM_HEAD_DIM, D_STATE),
                tail[:, 8 - (CONV_W - 1):, :].reshape(1, nb, CONV_W - 1, CONV_DIM))

    ssm_p5, conv_p4 = states(ssm_p, tail_p, bp)
    ssm_s5, conv_s4 = states(ssm_s, tail_s, bs)
    return (y_p, y_s, ssm_p5, conv_p4, k_p, v_p, ssm_s5, conv_s4, k_s, v_s)
```

```python
import functools

import jax
import jax.numpy as jnp
from jax import lax
from jax.experimental import pallas as pl
from jax.experimental.pallas import tpu as pltpu

D_MODEL = 1024
CHUNK = 64
D_INNER = 2048
SSM_HEAD_DIM = 64
SSM_HEADS = 32
N_GROUPS = 4
D_STATE = 128
GROUP_W = D_INNER // N_GROUPS
BC_W = N_GROUPS * D_STATE
CONV_W = 4
CONV_DIM = D_INNER + 2 * BC_W
ATT_HEADS = 16
ATT_HEAD_DIM = 64
D_ATT = 1024
LEFT_CHUNKS = 8
BAND_CHUNKS = LEFT_CHUNKS + 1
BAND = BAND_CHUNKS * CHUNK
REL_CLIP = 256
NORM_EPS = 1e-5
NEG_INF = -1e30
LOG2E = 1.4426950408889634

COL_Z = 0
COL_X = 2048
COL_B = 4096
COL_C = 4608
COL_Q = 5120
COL_V = 6144
COL_G = 7168
COL_GS = 8192
COL_GA = 9216
P_COLS = 10240
DT_PAD = 128

VMEM_LIMIT = 56 * 1024 * 1024

F32 = jnp.float32
BF16 = jnp.bfloat16


def _sigmoid(x):
    return 0.5 * jnp.tanh(0.5 * x) + 0.5


def _silu(x):
    h = 0.5 * x
    return h * jnp.tanh(h) + h


def _softplus(x):
    return jnp.maximum(x, 0.0) + jnp.log(1.0 + jnp.exp(-jnp.abs(x)))


PROJ_TM = 1024
PROJ_TN = 2560


def _proj_kernel(x_ref, nw_ref, wt_ref, wdt_ref, wkt_ref, p_ref, dt_ref, kt_ref, xn_ref):
    @pl.when(pl.program_id(1) == 0)
    def _():
        x = x_ref[...]
        ms = jnp.mean(x * x, axis=-1, keepdims=True)
        xn = (x * lax.rsqrt(ms + NORM_EPS)) * nw_ref[...]
        xn_ref[...] = xn.astype(BF16)
        dt_ref[...] = jnp.dot(xn_ref[...], wdt_ref[...], preferred_element_type=F32)
        kt_ref[...] = lax.dot_general(wkt_ref[...], xn_ref[...], (((1,), (1,)), ((), ())),
                                      preferred_element_type=F32).astype(BF16)

    p_ref[...] = lax.dot_general(xn_ref[...], wt_ref[...], (((1,), (1,)), ((), ())),
                                 preferred_element_type=F32).astype(BF16)


def _proj(x2d, norm_w, w_main, w_dt, w_kt):
    n = x2d.shape[0]
    tm = min(PROJ_TM, n)
    return pl.pallas_call(
        _proj_kernel,
        grid=(n // tm, P_COLS // PROJ_TN),
        in_specs=[
            pl.BlockSpec((tm, D_MODEL), lambda i, j: (i, 0)),
            pl.BlockSpec((1, D_MODEL), lambda i, j: (0, 0)),
            pl.BlockSpec((PROJ_TN, D_MODEL), lambda i, j: (j, 0)),
            pl.BlockSpec((D_MODEL, DT_PAD), lambda i, j: (0, 0)),
            pl.BlockSpec((D_ATT, D_MODEL), lambda i, j: (0, 0)),
        ],
        out_specs=[
            pl.BlockSpec((tm, PROJ_TN), lambda i, j: (i, j)),
            pl.BlockSpec((tm, DT_PAD), lambda i, j: (i, 0)),
            pl.BlockSpec((D_ATT, tm), lambda i, j: (0, i)),
        ],
        out_shape=[
            jax.ShapeDtypeStruct((n, P_COLS), BF16),
            jax.ShapeDtypeStruct((n, DT_PAD), F32),
            jax.ShapeDtypeStruct((D_ATT, n), BF16),
        ],
        scratch_shapes=[pltpu.VMEM((tm, D_MODEL), BF16)],
        compiler_params=pltpu.CompilerParams(
            dimension_semantics=("arbitrary", "arbitrary"),
            vmem_limit_bytes=VMEM_LIMIT),
        name="proj",
    )(x2d, norm_w, w_main, w_dt, w_kt)


SSD_SUB = 4


def _shift_matrix():
    t = lax.broadcasted_iota(jnp.int32, (CHUNK, (CONV_W - 1) * CHUNK), 0)
    c = lax.broadcasted_iota(jnp.int32, (CHUNK, (CONV_W - 1) * CHUNK), 1)
    return (c % CHUNK == t - (c // CHUNK + 1)).astype(BF16)


def _conv_silu(u_bf16, tail_ref, cols, w, b, shift_mat):
    u = u_bf16.astype(F32)
    taps = [w[CONV_W - 1 - s:CONV_W - s, :] for s in range(CONV_W)]
    delayed = jnp.concatenate([(taps[s] * u).astype(BF16) for s in range(1, CONV_W)], axis=0)
    acc = b + taps[0] * u + jnp.dot(shift_mat, delayed, preferred_element_type=F32)
    tail = tail_ref[:, cols]
    row8 = lax.broadcasted_iota(jnp.int32, (8, 1), 0)
    head = jnp.zeros_like(tail)
    for s in range(1, CONV_W):
        head = head + jnp.where(row8 < s, taps[s] * pltpu.roll(tail, s, axis=0), 0.0)
    acc = jnp.concatenate([acc[0:8, :] + head, acc[8:, :]], axis=0)
    tail_ref[:, cols] = u[CHUNK - 8:, :]
    return _silu(acc)


def _ssd_kernel(z_ref, x_ref, b_ref, c_ref, dtr_ref, h0_ref, conv0_ref,
                cw_ref, cb_ref, dtb_ref, alog_ref, dskip_ref, nw_ref,
                yz_ref, hout_ref, tailout_ref,
                ht_ref, tail_ref, xw_ref, y_ref, ea_ref, *, nsub):
    c = pl.program_id(1)
    n_chunks = pl.num_programs(1)

    @pl.when(c == 0)
    def _():
        for g in range(N_GROUPS):
            sl = slice(g * GROUP_W, (g + 1) * GROUP_W)
            ht_ref[:, sl] = h0_ref[0, sl, :].T
        tail_ref[...] = conv0_ref[0]

    def chunk(rows):
        cx = slice(0, D_INNER)
        cbm = slice(D_INNER, D_INNER + BC_W)
        ccm = slice(D_INNER + BC_W, CONV_DIM)
        shift_mat = _shift_matrix()
        xs = _conv_silu(x_ref[rows, :], tail_ref, cx, cw_ref[:, cx], cb_ref[:, cx], shift_mat)
        bm = _conv_silu(b_ref[rows, :], tail_ref, cbm, cw_ref[:, cbm], cb_ref[:, cbm], shift_mat)
        cm = _conv_silu(c_ref[rows, :], tail_ref, ccm, cw_ref[:, ccm], cb_ref[:, ccm], shift_mat)
        bm_b = bm.astype(BF16)
        cm_b = cm.astype(BF16)

        dt = _softplus(dtr_ref[rows, :] + dtb_ref[...])
        adt = dt * (-jnp.exp(alog_ref[...]) * LOG2E)
        li = lax.broadcasted_iota(jnp.int32, (CHUNK, CHUNK), 0)
        si = lax.broadcasted_iota(jnp.int32, (CHUNK, CHUNK), 1)
        tri = (li >= si).astype(F32)
        acs = jnp.dot(tri, adt, precision=lax.Precision.HIGHEST,
                      preferred_element_type=F32)
        s2 = lax.broadcasted_iota(jnp.int32, (CHUNK, 2 * CHUNK), 0)
        l2 = lax.broadcasted_iota(jnp.int32, (CHUNK, 2 * CHUNK), 1) % CHUNK
        tri_t2 = (l2 >= s2).astype(F32)
        acs_t2 = lax.dot_general(adt, tri_t2, (((0,), (0,)), ((), ())),
                                 precision=lax.Precision.HIGHEST,
                                 preferred_element_type=F32)

        lane = lax.broadcasted_iota(jnp.int32, (CHUNK, 2 * CHUNK), 1)
        lo = lane < CHUNK
        lo_row = lo[0:1, :]
        tri2 = (lax.broadcasted_iota(jnp.int32, (CHUNK, 2 * CHUNK), 0) >= (lane % CHUNK))

        yoff = []
        cb2 = []
        for g in range(N_GROUPS):
            sl = slice(g * GROUP_W, (g + 1) * GROUP_W)
            ns = slice(g * D_STATE, (g + 1) * D_STATE)
            yoff.append(jnp.dot(cm_b[:, ns], ht_ref[:, sl].astype(BF16),
                                preferred_element_type=F32))
            b2 = jnp.concatenate([bm_b[:, ns], bm_b[:, ns]], axis=0)
            cb2.append(lax.dot_general(cm_b[:, ns], b2, (((1,), (1,)), ((), ())),
                                       preferred_element_type=F32))

        for j in range(SSM_HEADS // 2):
            g = j // (SSM_HEADS // 2 // N_GROUPS)
            r0, r1 = 2 * j, 2 * j + 1
            sl = slice(j * 128, (j + 1) * 128)
            gsl = slice((j % 4) * 128, (j % 4 + 1) * 128)
            u_col = jnp.where(lo, acs[:, r0:r0 + 1], acs[:, r1:r1 + 1])
            v_row = jnp.where(lo_row, acs_t2[r0:r0 + 1, :], acs_t2[r1:r1 + 1, :])
            dtp = jnp.where(lo, dt[:, r0:r0 + 1], dt[:, r1:r1 + 1])
            decay = jnp.exp2(jnp.where(tri2, u_col - v_row, -jnp.inf))
            m = (cb2[g] * decay).astype(BF16)
            xs_p = xs[:, sl]
            xdt = xs_p * dtp
            xdt_b = xdt.astype(BF16)
            zero = jnp.zeros_like(xdt_b)
            rhs = jnp.concatenate([jnp.where(lo, xdt_b, zero), jnp.where(lo, zero, xdt_b)], axis=0)
            yd = jnp.dot(m, rhs, preferred_element_type=F32)
            e_u = jnp.exp2(u_col)
            y = yd + yoff[g][:, gsl] * e_u + dskip_ref[:, sl] * xs_p
            to_end = jnp.exp2(u_col[CHUNK - 1:CHUNK, :] - u_col)
            xw_ref[:, sl] = (xdt * to_end).astype(BF16)
            ea_ref[:, sl] = e_u[CHUNK - 8:CHUNK, :]
            zf = z_ref[rows, sl].astype(F32)
            y_ref[:, sl] = y * _silu(zf)

        for g in range(N_GROUPS):
            sl = slice(g * GROUP_W, (g + 1) * GROUP_W)
            ns = slice(g * D_STATE, (g + 1) * D_STATE)
            yg = y_ref[:, sl]
            ms = jnp.mean(yg * yg, axis=-1, keepdims=True)
            yz_ref[rows, sl] = ((yg * lax.rsqrt(ms + NORM_EPS)) * nw_ref[:, sl]).astype(BF16)
            upd = lax.dot_general(bm_b[:, ns], xw_ref[:, sl], (((0,), (0,)), ((), ())),
                                  preferred_element_type=F32)
            ht_ref[:, sl] = ht_ref[:, sl] * ea_ref[7:8, sl] + upd

    if nsub == 1:
        chunk(slice(None))
    else:
        def sub(i, carry):
            chunk(pl.ds(pl.multiple_of(i * CHUNK, CHUNK), CHUNK))
            return carry
        lax.fori_loop(0, nsub, sub, 0)

    @pl.when(c == n_chunks - 1)
    def _():
        for g in range(N_GROUPS):
            sl = slice(g * GROUP_W, (g + 1) * GROUP_W)
            hout_ref[0, sl, :] = ht_ref[:, sl].T
        tailout_ref[0] = tail_ref[...]


def _ssd(p, dtraw, h0, conv0, conv_w, conv_b, dt_bias, a_log, d_skip_x, ssm_norm_w, nb, nc):
    n = nb * nc * CHUNK
    nsub = SSD_SUB if nc % SSD_SUB == 0 else 1
    ns = nc // nsub
    tr = nsub * CHUNK
    row = lambda b, c: b * ns + c
    const = lambda b, c: (0, 0)
    assert h0.shape[0] == conv0.shape[0] and h0.shape[0] in (1, nb)
    state_row = (lambda b: b) if h0.shape[0] == nb else (lambda b: 0)
    return pl.pallas_call(
        functools.partial(_ssd_kernel, nsub=nsub),
        grid=(nb, ns),
        in_specs=[
            pl.BlockSpec((tr, D_INNER), lambda b, c: (row(b, c), COL_Z // D_INNER)),
            pl.BlockSpec((tr, D_INNER), lambda b, c: (row(b, c), COL_X // D_INNER)),
            pl.BlockSpec((tr, BC_W), lambda b, c: (row(b, c), COL_B // BC_W)),
            pl.BlockSpec((tr, BC_W), lambda b, c: (row(b, c), COL_C // BC_W)),
            pl.BlockSpec((tr, DT_PAD), lambda b, c: (row(b, c), 0)),
            pl.BlockSpec((1, D_INNER, D_STATE), lambda b, c: (state_row(b), 0, 0)),
            pl.BlockSpec((1, 8, CONV_DIM), lambda b, c: (state_row(b), 0, 0)),
            pl.BlockSpec((CONV_W, CONV_DIM), const),
            pl.BlockSpec((1, CONV_DIM), const),
            pl.BlockSpec((1, DT_PAD), const),
            pl.BlockSpec((1, DT_PAD), const),
            pl.BlockSpec((1, D_INNER), const),
            pl.BlockSpec((1, D_INNER), const),
        ],
        out_specs=[
            pl.BlockSpec((tr, D_INNER), lambda b, c: (row(b, c), 0)),
            pl.BlockSpec((1, D_INNER, D_STATE), lambda b, c: (b, 0, 0)),
            pl.BlockSpec((1, 8, CONV_DIM), lambda b, c: (b, 0, 0)),
        ],
        out_shape=[
            jax.ShapeDtypeStruct((n, D_INNER), BF16),
            jax.ShapeDtypeStruct((nb, D_INNER, D_STATE), F32),
            jax.ShapeDtypeStruct((nb, 8, CONV_DIM), F32),
        ],
        scratch_shapes=[
            pltpu.VMEM((D_STATE, D_INNER), F32),
            pltpu.VMEM((8, CONV_DIM), F32),
            pltpu.VMEM((CHUNK, D_INNER), BF16),
            pltpu.VMEM((CHUNK, D_INNER), F32),
            pltpu.VMEM((8, D_INNER), F32),
        ],
        compiler_params=pltpu.CompilerParams(
            dimension_semantics=("arbitrary", "arbitrary"),
            vmem_limit_bytes=VMEM_LIMIT),
        name="ssd",
    )(p, p, p, p, dtraw, h0, conv0, conv_w, conv_b, dt_bias, a_log, d_skip_x, ssm_norm_w)


ATT_TQ = 256
ATT_TILES = 3
ATT_KEYS = ATT_TILES * ATT_TQ
BIAS_ROW = 1024
KEEP_TILES = LEFT_CHUNKS * CHUNK // ATT_TQ


def _bias_kernel(r_ref, o_ref):
    x = jnp.broadcast_to(r_ref[0], (ATT_TQ, BIAS_ROW))
    tab = pltpu.roll(x, BIAS_ROW - ATT_TQ, axis=1, stride=1, stride_axis=0)[:, :ATT_KEYS]
    qc = lax.broadcasted_iota(jnp.int32, (ATT_TQ, ATT_KEYS), 0) // CHUNK
    kc = lax.broadcasted_iota(jnp.int32, (ATT_TQ, ATT_KEYS), 1) // CHUNK
    ok = (kc >= qc) & (kc <= qc + LEFT_CHUNKS)
    o_ref[0] = jnp.where(ok, tab * LOG2E, NEG_INF)


def _bias_table(rel_row):
    return pl.pallas_call(
        _bias_kernel,
        grid=(ATT_HEADS,),
        in_specs=[pl.BlockSpec((1, 1, BIAS_ROW), lambda h: (h, 0, 0))],
        out_specs=pl.BlockSpec((1, ATT_TQ, ATT_KEYS), lambda h: (h, 0, 0)),
        out_shape=jax.ShapeDtypeStruct((ATT_HEADS, ATT_TQ, ATT_KEYS), F32),
        name="bias_table",
    )(rel_row)


def _softmax_pv(s_tiles, v_tiles):
    def lane_reduce(tiles, combine, reduce):
        if len({t.shape for t in tiles}) == 1:
            tiles = [functools.reduce(combine, tiles)]
        return functools.reduce(combine, [reduce(t, axis=-1, keepdims=True) for t in tiles])

    mx = lane_reduce(s_tiles, jnp.maximum, jnp.max)
    p_tiles = [jnp.exp2(s - mx) for s in s_tiles]
    den = lane_reduce(p_tiles, jnp.add, jnp.sum)
    o = functools.reduce(jnp.add, [jnp.dot(p.astype(BF16), v, preferred_element_type=F32)
                                   for p, v in zip(p_tiles, v_tiles)])
    return o / den


def _attn_prompt_kernel(q_ref, g_ref, kt0_ref, kt1_ref, kt2_ref, v0_ref, v1_ref, v2_ref,
                        bias_ref, o_ref, kp_ref, vp_ref):
    t = pl.program_id(1)
    kt_refs = (kt0_ref, kt1_ref, kt2_ref)
    v_refs = (v0_ref, v1_ref, v2_ref)

    @pl.when(t >= pl.num_programs(1) - KEEP_TILES)
    def _():
        kp_ref[0] = kt2_ref[...].astype(F32).T
        vp_ref[0] = v2_ref[...].astype(F32)

    def body(mask_start):
        lo = lax.broadcasted_iota(jnp.int32, (ATT_TQ, 2 * ATT_HEAD_DIM), 1) < ATT_HEAD_DIM
        for hp in range(ATT_HEADS // 2):
            sl = slice(hp * 128, (hp + 1) * 128)
            qp = q_ref[:, sl]
            zero = jnp.zeros_like(qp)
            q2 = jnp.concatenate([jnp.where(lo, qp, zero), jnp.where(lo, zero, qp)], axis=0)
            s_tiles = []
            for n in range(ATT_TILES):
                ks = slice(n * ATT_TQ, (n + 1) * ATT_TQ)
                s = jnp.dot(q2, kt_refs[n][sl, :], preferred_element_type=F32)
                s = s + jnp.concatenate([bias_ref[2 * hp, :, ks], bias_ref[2 * hp + 1, :, ks]],
                                        axis=0)
                if mask_start and n < ATT_TILES - 1:
                    s = jnp.where(t + n >= ATT_TILES - 1, s, NEG_INF)
                s_tiles.append(s)
            o2 = _softmax_pv(s_tiles, [v_refs[n][:, sl] for n in range(ATT_TILES)])
            o_pair = jnp.where(lo, o2[:ATT_TQ], o2[ATT_TQ:])
            gate = g_ref[:, sl].astype(F32)
            o_ref[:, sl] = (o_pair * _silu(gate)).astype(BF16)

    body(True)


def _attn_prompt(p, kt, bias, nb, seq):
    nt = seq // ATT_TQ

    def tile_row(n):
        return lambda b, t: b * nt + jnp.maximum(t - (ATT_TILES - 1) + n, 0)

    in_specs = [
        pl.BlockSpec((ATT_TQ, D_ATT), lambda b, t: (b * nt + t, COL_Q // D_ATT)),
        pl.BlockSpec((ATT_TQ, D_ATT), lambda b, t: (b * nt + t, COL_G // D_ATT)),
    ]
    in_specs += [pl.BlockSpec((D_ATT, ATT_TQ), lambda b, t, r=tile_row(n): (0, r(b, t)))
                 for n in range(ATT_TILES)]
    in_specs += [pl.BlockSpec((ATT_TQ, D_ATT), lambda b, t, r=tile_row(n): (r(b, t), COL_V // D_ATT))
                 for n in range(ATT_TILES)]
    in_specs += [pl.BlockSpec((ATT_HEADS, ATT_TQ, ATT_KEYS), lambda b, t: (0, 0, 0),
                              pipeline_mode=pl.Buffered(1))]
    keep_spec = pl.BlockSpec((1, ATT_TQ, D_ATT),
                             lambda b, t: (b, jnp.maximum(t - (nt - KEEP_TILES), 0), 0))
    keep_shape = jax.ShapeDtypeStruct((nb, KEEP_TILES * ATT_TQ, D_ATT), F32)
    return pl.pallas_call(
        _attn_prompt_kernel,
        grid=(nb, nt),
        in_specs=in_specs,
        out_specs=[pl.BlockSpec((ATT_TQ, D_ATT), lambda b, t: (b * nt + t, 0)),
                   keep_spec, keep_spec],
        out_shape=[jax.ShapeDtypeStruct((nb * seq, D_ATT), BF16), keep_shape, keep_shape],
        compiler_params=pltpu.CompilerParams(
            dimension_semantics=("arbitrary", "arbitrary"),
            vmem_limit_bytes=VMEM_LIMIT),
        name="attn_prompt",
    )(p, p, kt, kt, kt, p, p, p, bias)


def _attn_step_kernel(q_ref, g_ref, kn_ref, vn_ref, kc_ref, vc_ref, bias_ref,
                      o_ref, ks_ref, vs_ref):
    keep = LEFT_CHUNKS * CHUNK
    ks_ref[0, 0:keep - CHUNK, :] = kc_ref[0, CHUNK:keep, :]
    ks_ref[0, keep - CHUNK:keep, :] = kn_ref[...].astype(F32)
    vs_ref[0, 0:keep - CHUNK, :] = vc_ref[0, CHUNK:keep, :]
    vs_ref[0, keep - CHUNK:keep, :] = vn_ref[...].astype(F32)
    lo = lax.broadcasted_iota(jnp.int32, (CHUNK, 2 * ATT_HEAD_DIM), 1) < ATT_HEAD_DIM

    for hp in range(ATT_HEADS // 2):
        sl = slice(hp * 128, (hp + 1) * 128)
        qp = q_ref[:, sl]
        k_tiles = [kc_ref[0, :, sl].astype(BF16), kn_ref[:, sl]]
        v_tiles = [vc_ref[0, :, sl].astype(BF16), vn_ref[:, sl]]
        zero = jnp.zeros_like(qp)
        q2 = jnp.concatenate([jnp.where(lo, qp, zero), jnp.where(lo, zero, qp)], axis=0)
        s_tiles = []
        for n, (k0, k1) in enumerate(((0, keep), (keep, BAND))):
            s = lax.dot_general(q2, k_tiles[n], (((1,), (1,)), ((), ())),
                                preferred_element_type=F32)
            s_tiles.append(s + jnp.concatenate([bias_ref[2 * hp, 0:CHUNK, k0:k1],
                                                bias_ref[2 * hp + 1, 0:CHUNK, k0:k1]], axis=0))
        o2 = _softmax_pv(s_tiles, v_tiles)
        o_pair = jnp.where(lo, o2[:CHUNK], o2[CHUNK:])
        gate = g_ref[:, sl].astype(F32)
        o_ref[:, sl] = (o_pair * _silu(gate)).astype(BF16)


def _attn_step(p, k_new, cache_k, cache_v, bias, nb):
    keep = LEFT_CHUNKS * CHUNK
    new = lambda col: pl.BlockSpec((CHUNK, D_ATT), lambda b: (b, col))
    cache = pl.BlockSpec((1, keep, D_ATT), lambda b: (b, 0, 0))
    cache_shape = jax.ShapeDtypeStruct((nb, keep, D_ATT), F32)
    return pl.pallas_call(
        _attn_step_kernel,
        grid=(nb,),
        in_specs=[new(COL_Q // D_ATT), new(COL_G // D_ATT), new(0),
                  new(COL_V // D_ATT), cache, cache,
                  pl.BlockSpec((ATT_HEADS, ATT_TQ, ATT_KEYS), lambda b: (0, 0, 0),
                               pipeline_mode=pl.Buffered(1))],
        out_specs=[pl.BlockSpec((CHUNK, D_ATT), lambda b: (b, 0)), cache, cache],
        out_shape=[jax.ShapeDtypeStruct((nb * CHUNK, D_ATT), BF16), cache_shape, cache_shape],
        compiler_params=pltpu.CompilerParams(
            dimension_semantics=("arbitrary",),
            vmem_limit_bytes=VMEM_LIMIT),
        name="attn_step",
    )(p, p, k_new, p, cache_k, cache_v, bias)


OUT_TM = 1024


def _out_kernel(yz_ref, og_ref, gs_ref, ga_ref, x_ref, wos_ref, woa_ref, wo_ref, fnw_ref, y_ref):
    y_ssm = jnp.dot(yz_ref[...], wos_ref[...], preferred_element_type=F32)
    y_att = jnp.dot(og_ref[...], woa_ref[...], preferred_element_type=F32)
    merged = (_sigmoid(gs_ref[...].astype(F32)) * y_ssm
              + _sigmoid(ga_ref[...].astype(F32)) * y_att)
    h = x_ref[...] + jnp.dot(merged.astype(BF16), wo_ref[...], preferred_element_type=F32)
    ms = jnp.mean(h * h, axis=-1, keepdims=True)
    y_ref[...] = (h * lax.rsqrt(ms + NORM_EPS)) * fnw_ref[...]


def _out(yz, og, p, x2d, w_out_ssm, w_out_att, w_o, final_norm_w):
    n = x2d.shape[0]
    tm = min(OUT_TM, n)
    const = lambda i: (0, 0)
    return pl.pallas_call(
        _out_kernel,
        grid=(n // tm,),
        in_specs=[
            pl.BlockSpec((tm, D_INNER), lambda i: (i, 0)),
            pl.BlockSpec((tm, D_ATT), lambda i: (i, 0)),
            pl.BlockSpec((tm, D_MODEL), lambda i: (i, COL_GS // D_MODEL)),
            pl.BlockSpec((tm, D_MODEL), lambda i: (i, COL_GA // D_MODEL)),
            pl.BlockSpec((tm, D_MODEL), lambda i: (i, 0)),
            pl.BlockSpec((D_INNER, D_MODEL), const, pipeline_mode=pl.Buffered(1)),
            pl.BlockSpec((D_ATT, D_MODEL), const, pipeline_mode=pl.Buffered(1)),
            pl.BlockSpec((D_MODEL, D_MODEL), const, pipeline_mode=pl.Buffered(1)),
            pl.BlockSpec((1, D_MODEL), const),
        ],
        out_specs=pl.BlockSpec((tm, D_MODEL), lambda i: (i, 0)),
        out_shape=jax.ShapeDtypeStruct((n, D_MODEL), F32),
        compiler_params=pltpu.CompilerParams(
            dimension_semantics=("arbitrary",),
            vmem_limit_bytes=VMEM_LIMIT),
        name="out",
    )(yz, og, p, p, x2d, w_out_ssm, w_out_att, w_o, final_norm_w)


def _rel_bias_rows(rel_bias):
    d = ATT_TILES * ATT_TQ - jnp.arange(BIAS_ROW)
    idx = jnp.clip(d, -REL_CLIP, REL_CLIP) + REL_CLIP
    return rel_bias[:, idx].astype(F32).reshape(ATT_HEADS, 1, BIAS_ROW)


def kernel(x_prompt, x_sample, state_ssm, state_conv, cache_k, cache_v, norm_w, w_in, conv_w,
           conv_b, dt_bias, a_log, d_skip, ssm_norm_w, w_out_ssm, rel_bias, w_out_att, w_o,
           final_norm_w):
    bp, lp, _ = x_prompt.shape
    bs, ls, _ = x_sample.shape
    ncp = lp // CHUNK
    assert ls == CHUNK and cache_k.shape[2] == LEFT_CHUNKS * CHUNK

    wt = w_in[0].T
    o = 0
    parts = {}
    for name, size in (("z", D_INNER), ("xbc", CONV_DIM), ("dt", SSM_HEADS), ("q", D_ATT),
                       ("k", D_ATT), ("v", D_ATT), ("g", D_ATT), ("gs", D_MODEL), ("ga", D_MODEL)):
        parts[name] = wt[o:o + size]
        o += size
    w_main = jnp.concatenate(
        [parts["z"], parts["xbc"], parts["q"] * (ATT_HEAD_DIM ** -0.5 * LOG2E), parts["v"],
         parts["g"], parts["gs"], parts["ga"]], axis=0).astype(BF16)
    w_kt = parts["k"].astype(BF16)
    w_dt = jnp.pad(parts["dt"].T, ((0, 0), (0, DT_PAD - SSM_HEADS))).astype(BF16)
    nw = norm_w[0].reshape(1, D_MODEL)
    cw = conv_w[0]
    cb = conv_b[0].reshape(1, CONV_DIM)
    dtb = jnp.pad(dt_bias[0], (0, DT_PAD - SSM_HEADS)).reshape(1, DT_PAD)
    alog = jnp.pad(a_log[0], (0, DT_PAD - SSM_HEADS)).reshape(1, DT_PAD)
    dskip_x = jnp.repeat(d_skip[0], SSM_HEAD_DIM).reshape(1, D_INNER)
    snw = ssm_norm_w[0].reshape(1, D_INNER)
    wos = w_out_ssm[0].astype(BF16)
    woa = w_out_att[0].astype(BF16)
    wo = w_o[0].astype(BF16)
    fnw = final_norm_w.reshape(1, D_MODEL)
    bias = _bias_table(_rel_bias_rows(rel_bias[0]))

    ssd_params = (cw, cb, dtb, alog, dskip_x, snw)
    out_params = (wos, woa, wo, fnw)

    xp = x_prompt.reshape(bp * lp, D_MODEL)
    p_p, dt_p, kt_p = _proj(xp, nw, w_main, w_dt, w_kt)
    h0_p = jnp.zeros((1, D_INNER, D_STATE), F32)
    conv0_p = jnp.zeros((1, 8, CONV_DIM), F32)
    yz_p, ssm_p, tail_p = _ssd(p_p, dt_p, h0_p, conv0_p, *ssd_params, nb=bp, nc=ncp)
    og_p, k_p, v_p = _attn_prompt(p_p, kt_p, bias, nb=bp, seq=lp)
    y_p = _out(yz_p, og_p, p_p, xp, *out_params).reshape(bp, lp, D_MODEL)
    keep = LEFT_CHUNKS * CHUNK
    k_p = k_p.reshape(1, bp, keep, ATT_HEADS, ATT_HEAD_DIM)
    v_p = v_p.reshape(1, bp, keep, ATT_HEADS, ATT_HEAD_DIM)

    xs = x_sample.reshape(bs * ls, D_MODEL)
    p_s, dt_s, kt_s = _proj(xs, nw, w_main, w_dt, w_kt)
    h0_s = state_ssm[0].reshape(bs, D_INNER, D_STATE)
    conv0_s = jnp.pad(state_conv[0], ((0, 0), (8 - (CONV_W - 1), 0), (0, 0)))
    yz_s, ssm_s, tail_s = _ssd(p_s, dt_s, h0_s, conv0_s, *ssd_params, nb=bs, nc=1)
    og_s, k_s, v_s = _attn_step(p_s, kt_s.T, cache_k[0].reshape(bs, keep, D_ATT),
                                cache_v[0].reshape(bs, keep, D_ATT), bias, nb=bs)
    y_s = _out(yz_s, og_s, p_s, xs, *out_params).reshape(bs, ls, D_MODEL)
    k_s = k_s.reshape(1, bs, keep, ATT_HEADS, ATT_HEAD_DIM)
    v_s = v_s.reshape(1, bs, keep, ATT_HEADS, ATT_HEAD_DIM)

    def states(ssm, tail, nb):
        return (ssm.reshape(1, nb, SSM_HEADS, SSM_HEAD_DIM, D_STATE),
                tail[:, 8 - (CONV_W - 1):, :].reshape(1, nb, CONV_W - 1, CONV_DIM))

    ssm_p5, conv_p4 = states(ssm_p, tail_p, bp)
    ssm_s5, conv_s4 = states(ssm_s, tail_s, bs)
    return (y_p, y_s, ssm_p5, conv_p4, k_p, v_p, ssm_s5, conv_s4, k_s, v_s)
```
